```python
import math
import jax
import jax.numpy as jnp
from jax import lax
import numpy as np


D_MODEL = 1024
BATCH = 1
SEQ = 16384
DEPTH = 2

GRID_W = 64
CTX_LEN = 256
ROPE_BASE = 10000.0
CONV_K = 5
CHUNK = 64
Q_BLOCK = 128
EPS = 1e-6
N_BRANCH = 4
BRANCH_W = D_MODEL // 2

DIFF_H = 4
DIFF_HD = 64
DIFF_QK = DIFF_H * 2 * DIFF_HD
DIFF_V = DIFF_H * 2 * DIFF_HD
ML_H = 4
ML_DK = 64
ML_DV = 128
ML_QK = 2 * ML_H * ML_DK
ML_V = ML_H * ML_DV
ML_GATES = 4 * ML_H
MLA_H = 4
MLA_Q_RANK = 384
MLA_KV_RANK = 256
MLA_NOPE = 64
MLA_ROPE = 32
MLA_V = 128
SSD_H = 8
SSD_P = 64
SSD_G = 2
SSD_N = 64
D_INNER = SSD_H * SSD_P
SSD_XBC = D_INNER + 2 * SSD_G * SSD_N
SSD_DT = 2 * SSD_H
D_FF = 4 * D_MODEL
DN_ALPHA = (2 * DEPTH) ** 0.25
DN_BETA = (8 * DEPTH) ** -0.25

IN_SPLITS = (DIFF_QK, DIFF_QK, DIFF_V,
             ML_QK, ML_V, ML_V, ML_GATES,
             MLA_Q_RANK, MLA_KV_RANK, MLA_ROPE,
             D_INNER, SSD_XBC, SSD_DT)
IN_COLS = sum(IN_SPLITS)

kernel_name = 'hybrid_gated_diffusion_block'


def layer_norm(x, g=None, b=None):
    xf = x.astype(jnp.float32)
    mu = jnp.mean(xf, -1, keepdims=True)
    var = jnp.mean(jnp.square(xf - mu), -1, keepdims=True)
    y = (xf - mu) * lax.rsqrt(var + EPS)
    if g is not None:
        y = y * g + b
    return y.astype(x.dtype)


def rms_norm(x, w):
    xf = x.astype(jnp.float32)
    y = xf * lax.rsqrt(jnp.mean(jnp.square(xf), -1, keepdims=True) + EPS) * w
    return y.astype(x.dtype)


def modulate(x, shift, scale):
    return layer_norm(x) * (1.0 + scale) + shift


def rope_axis(x, pos):
    d = x.shape[-1]
    half = d // 2
    inv = ROPE_BASE ** (-jnp.arange(half, dtype=jnp.float32) * 2.0 / d)
    ang = pos.astype(jnp.float32)[:, None] * inv[None, :]
    bshape = (pos.shape[0],) + (1,) * (x.ndim - 3) + (half,)
    cos = jnp.cos(ang).reshape(bshape)
    sin = jnp.sin(ang).reshape(bshape)
    xf = x.astype(jnp.float32)
    x1, x2 = xf[..., :half], xf[..., half:]
    return jnp.concatenate([x1 * cos - x2 * sin, x2 * cos + x1 * sin], -1).astype(x.dtype)


def rope_2d(x, row, col):
    half = x.shape[-1] // 2
    return jnp.concatenate([rope_axis(x[..., :half], row), rope_axis(x[..., half:], col)], -1)


def dwconv(x, w, b):
    ch = x.shape[-1]
    y = lax.conv_general_dilated(x, w[:, None, :].astype(x.dtype), window_strides=(1,),
                                 padding=[(CONV_K // 2, CONV_K // 2)],
                                 dimension_numbers=('NWC', 'WIO', 'NWC'), feature_group_count=ch)
    return y + b


def to_chunks(a):
    bsz, t = a.shape[:2]
    return jnp.moveaxis(a.reshape((bsz, t // CHUNK, CHUNK) + a.shape[2:]), 1, 0)


def from_chunks(a):
    a = jnp.moveaxis(a, 0, 1)
    return a.reshape((a.shape[0], a.shape[1] * a.shape[2]) + a.shape[3:])


def over_query_blocks(fn, *qs):
    bsz, t = qs[0].shape[:2]
    nb = t // Q_BLOCK
    blocks = tuple(jnp.moveaxis(q.reshape((bsz, nb, Q_BLOCK) + q.shape[2:]), 1, 0) for q in qs)
    out = lax.map(lambda a: fn(*a), blocks)
    out = jnp.moveaxis(out, 0, 1)
    return out.reshape((bsz, t) + out.shape[3:])


def softmax_attend(q, k, v, scale):
    s = jnp.einsum('bqhd,bkhd->bhqk', q, k).astype(jnp.float32) * scale
    p = jax.nn.softmax(s, axis=-1).astype(v.dtype)
    o = jnp.einsum('bhqk,bkhe->bqhe', p, v)
    return o.reshape(o.shape[:2] + (-1,))


def flip_t(a):
    return jnp.flip(a, axis=1)


def bidirectional_scan(scan_fn, ctx_dirs, lat_dirs, state0):
    y_ctx, y_lat = 0.0, 0.0
    for d in range(2):
        ca, la = ctx_dirs[d], lat_dirs[d]
        if d == 1:
            ca = tuple(flip_t(a) for a in ca)
            la = tuple(flip_t(a) for a in la)
        yc, st = scan_fn(*ca, state0)
        yl, _ = scan_fn(*la, st)
        if d == 1:
            yc, yl = flip_t(yc), flip_t(yl)
        y_ctx = y_ctx + yc
        y_lat = y_lat + yl
    return y_ctx, y_lat


def mlstm_chunk_scan(q, k, v, ig, lf, state):
    dtype = v.dtype
    causal = jnp.tril(jnp.ones((CHUNK, CHUNK), bool))[None, :, :, None]

    def body(carry, chunk):
        cmat, nvec, m = carry
        qc, kc, vc, ic, fc = chunk
        g = jnp.cumsum(fc, axis=1)
        logw = jnp.where(causal, g[:, :, None] - g[:, None] + ic[:, None], -jnp.inf)
        log_inter = g + m[:, None]
        m_j = jnp.maximum(log_inter, jnp.max(logw, axis=2))
        w = jnp.exp(logw - m_j[:, :, None]) * jnp.einsum('bjhd,bshd->bjsh', qc, kc)
        a_inter = jnp.exp(log_inter - m_j)
        num = jnp.einsum('bjsh,bshe->bjhe', w, vc) + a_inter[..., None] * jnp.einsum('bjhd,bhde->bjhe', qc, cmat)
        den = jnp.sum(w, axis=2) + a_inter * jnp.einsum('bjhd,bhd->bjh', qc, nvec)
        h = num / jnp.maximum(jnp.abs(den), jnp.exp(-m_j))[..., None]
        g_end = g[:, -1]
        log_s = g_end[:, None] - g + ic
        m_new = jnp.maximum(g_end + m, jnp.max(log_s, axis=1))
        ws = jnp.exp(log_s - m_new[:, None])
        decay = jnp.exp(g_end + m - m_new)
        cmat = decay[..., None, None] * cmat + jnp.einsum('bsh,bshd,bshe->bhde', ws, kc, vc)
        nvec = decay[..., None] * nvec + jnp.einsum('bsh,bshd->bhd', ws, kc)
        return (cmat, nvec, m_new), h

    chunks = tuple(to_chunks(a.astype(jnp.float32)) for a in (q, k, v, ig, lf))
    state, h = lax.scan(body, state, chunks)
    return from_chunks(h).astype(dtype), state


def ssd_chunk_scan(x, dt, a, bm, cm, h0):
    dtype = x.dtype
    causal = jnp.tril(jnp.ones((CHUNK, CHUNK), bool))[None, :, :, None]

    def body(h, chunk):
        xc, dtc, ac, bc, cc = chunk
        s = jnp.cumsum(ac, axis=1)
        decay = jnp.exp(jnp.where(causal, s[:, :, None] - s[:, None], -jnp.inf))
        scores = decay * jnp.einsum('bjhn,bihn->bjih', cc, bc)
        xdt = xc * dtc[..., None]
        y = jnp.einsum('bjih,bihp->bjhp', scores, xdt) + jnp.exp(s)[..., None] * jnp.einsum('bjhn,bhpn->bjhp', cc, h)
        s_end = s[:, -1]
        h = jnp.exp(s_end)[..., None, None] * h + jnp.einsum('bih,bihp,bihn->bhpn', jnp.exp(s_end[:, None] - s), xdt, bc)
        return h, y

    chunks = tuple(to_chunks(t.astype(jnp.float32)) for t in (x, dt, a, bm, cm))
    h, y = lax.scan(body, h0, chunks)
    return from_chunks(y).astype(dtype), h


def diff_attention(lat, ctx, lam, subln_w, lam_init, row, col, ctx_out):
    def heads(q, k, v, pos):
        bsz, t = q.shape[:2]
        q = q.reshape(bsz, t, DIFF_H, 2, DIFF_HD)
        k = k.reshape(bsz, t, DIFF_H, 2, DIFF_HD)
        if pos is not None:
            q = rope_2d(q, *pos)
            k = rope_2d(k, *pos)
        return q, k, v.reshape(bsz, t, DIFF_H, 2 * DIFF_HD)

    lam = lam.astype(jnp.float32)
    lam_full = jnp.exp(jnp.sum(lam[0] * lam[1])) - jnp.exp(jnp.sum(lam[2] * lam[3])) + lam_init
    scale = DIFF_HD ** -0.5

    def attend(q, k, v):
        s = jnp.einsum('bqhcd,bkhcd->bhcqk', q, k).astype(jnp.float32) * scale
        p = jax.nn.softmax(s, axis=-1)
        amap = (p[:, :, 0] - lam_full * p[:, :, 1]).astype(v.dtype)
        o = jnp.einsum('bhqk,bkhe->bqhe', amap, v)
        o = rms_norm(o, subln_w) * (1.0 - lam_init)
        return o.reshape(o.shape[:2] + (-1,))

    q_l, k_l, v_l = heads(*lat, (row, col))
    q_c, k_c, v_c = heads(*ctx, None)
    k_all = jnp.concatenate([k_l, k_c], 1)
    v_all = jnp.concatenate([v_l, v_c], 1)
    y_lat = over_query_blocks(lambda qb: attend(qb, k_all, v_all), q_l)
    y_ctx = attend(q_c, k_c, v_c) if ctx_out else None
    return y_lat, y_ctx


def mla_mixer(lat, ctx, q_norm, kv_norm, w_uq, w_ukv, row, col, ctx_out):
    def qkv(cq, ckv, kr, pos):
        bsz, t = cq.shape[:2]
        q = (rms_norm(cq, q_norm) @ w_uq).reshape(bsz, t, MLA_H, MLA_NOPE + MLA_ROPE)
        kv = (rms_norm(ckv, kv_norm) @ w_ukv).reshape(bsz, t, MLA_H, MLA_NOPE + MLA_V)
        q_nope, q_rope = q[..., :MLA_NOPE], q[..., MLA_NOPE:]
        k_nope, v = kv[..., :MLA_NOPE], kv[..., MLA_NOPE:]
        kr = kr[:, :, None, :]
        if pos is not None:
            q_rope = rope_2d(q_rope, *pos)
            kr = rope_2d(kr, *pos)
        q = jnp.concatenate([q_nope, q_rope], -1)
        k = jnp.concatenate([k_nope, jnp.broadcast_to(kr, (bsz, t, MLA_H, MLA_ROPE))], -1)
        return q, k, v

    q_l, k_l, v_l = qkv(*lat, (row, col))
    q_c, k_c, v_c = qkv(*ctx, None)
    scale = (MLA_NOPE + MLA_ROPE) ** -0.5
    k_all = jnp.concatenate([k_l, k_c], 1)
    v_all = jnp.concatenate([v_l, v_c], 1)
    y_lat = over_query_blocks(lambda qb: softmax_attend(qb, k_all, v_all, scale), q_l)
    y_ctx = softmax_attend(q_c, k_c, v_c, scale) if ctx_out else None
    return y_lat, y_ctx


def mlstm_mixer(lat, ctx, conv_w, conv_b, gate_b, norm_w, ctx_out):
    def directions(qk, v, gates):
        bsz, t = qk.shape[:2]
        qk = jax.nn.silu(dwconv(qk, conv_w, conv_b))
        q = qk[..., :ML_H * ML_DK].reshape(bsz, t, ML_H, ML_DK)
        k = qk[..., ML_H * ML_DK:].reshape(bsz, t, ML_H, ML_DK) * (ML_DK ** -0.5)
        v = v.reshape(bsz, t, ML_H, ML_DV)
        gt = gates.reshape(bsz, t, 4, ML_H).astype(jnp.float32) + gate_b
        return [(q, k, v, gt[:, :, 2 * d], jax.nn.log_sigmoid(gt[:, :, 2 * d + 1])) for d in range(2)]

    def finish(h, o):
        bsz, t = h.shape[:2]
        h = rms_norm(h, norm_w.reshape(ML_H, ML_DV)).reshape(bsz, t, ML_V)
        return h * jax.nn.sigmoid(o)

    bsz = lat[0].shape[0]
    state0 = (jnp.zeros((bsz, ML_H, ML_DK, ML_DV), jnp.float32),
              jnp.zeros((bsz, ML_H, ML_DK), jnp.float32),
              jnp.zeros((bsz, ML_H), jnp.float32))
    h_ctx, h_lat = bidirectional_scan(mlstm_chunk_scan, directions(ctx[0], ctx[1], ctx[3]),
                                      directions(lat[0], lat[1], lat[3]), state0)
    y_lat = finish(h_lat, lat[2])
    y_ctx = finish(h_ctx, ctx[2]) if ctx_out else None
    return y_lat, y_ctx


def ssd_mixer(lat, ctx, conv_w, conv_b, dt_bias, a_log, d_skip, norm_w, ctx_out):
    a_mat = -jnp.exp(a_log.astype(jnp.float32))
    rep = SSD_H // SSD_G

    def prep(xbc, dt):
        bsz, t = xbc.shape[:2]
        xbc = jax.nn.silu(dwconv(xbc, conv_w, conv_b))
        xs = xbc[..., :D_INNER].reshape(bsz, t, SSD_H, SSD_P)
        bm = jnp.repeat(xbc[..., D_INNER:D_INNER + SSD_G * SSD_N].reshape(bsz, t, SSD_G, SSD_N), rep, axis=2)
        cm = jnp.repeat(xbc[..., D_INNER + SSD_G * SSD_N:].reshape(bsz, t, SSD_G, SSD_N), rep, axis=2)
        dt = jax.nn.softplus(dt.reshape(bsz, t, 2, SSD_H).astype(jnp.float32) + dt_bias)
        dirs = [(xs, dt[:, :, d], dt[:, :, d] * a_mat[d], bm, cm) for d in range(2)]
        return xs, dirs

    def finish(y, xs, z):
        bsz, t = y.shape[:2]
        y = (y + d_skip[:, None] * xs).reshape(bsz, t, D_INNER) * jax.nn.silu(z)
        y = rms_norm(y.reshape(bsz, t, SSD_G, D_INNER // SSD_G), norm_w.reshape(SSD_G, D_INNER // SSD_G))
        return y.reshape(bsz, t, D_INNER)

    xs_l, dirs_l = prep(lat[1], lat[2])
    xs_c, dirs_c = prep(ctx[1], ctx[2])
    h0 = jnp.zeros((lat[0].shape[0], SSD_H, SSD_P, SSD_N), jnp.float32)
    y_ctx, y_lat = bidirectional_scan(ssd_chunk_scan, dirs_c, dirs_l, h0)
    out_lat = finish(y_lat, xs_l, lat[0])
    out_ctx = finish(y_ctx, xs_c, ctx[0]) if ctx_out else None
    return out_lat, out_ctx


def gated_merge(h, branches, w_gate, b_gate, w_branch, w_o):
    y = 0.0
    for k, br in enumerate(branches):
        y = y + jax.nn.sigmoid(h @ w_gate[k] + b_gate[k]) * (br @ w_branch[k])
    return y @ w_o


def sq_relu_mlp(h, w_up, b_up, w_down, b_down):
    u = jax.nn.relu(h @ w_up + b_up)
    return (u * u) @ w_down + b_down


def setup_inputs(seed: int = 0) -> dict:
    key = jax.random.key(seed)
    ks = iter(jax.random.split(key, 48))
    f32 = jnp.float32
    L, D = DEPTH, D_MODEL

    def nrm(shape, std):
        return std * jax.random.normal(next(ks), shape, f32)

    def gain(shape):
        return 1.0 + nrm(shape, 0.02)

    fb = jnp.linspace(3.0, 6.0, ML_H, dtype=f32)
    zb = jnp.zeros((ML_H,), f32)
    dt0 = jnp.exp(jax.random.uniform(next(ks), (L, 2, SSD_H), f32, math.log(1e-3), math.log(1e-1)))
    return {
        'x': nrm((BATCH, SEQ, D), 1.0),
        'c': nrm((BATCH, D), 1.0),
        'ctx': nrm((BATCH, CTX_LEN, D), 1.0),
        'c_ctx': nrm((D,), 1.0),
        'w_mod': nrm((L, D, 6 * D), 0.5 * D ** -0.5),
        'b_mod': nrm((L, 6 * D), 0.01),
        'w_in': nrm((L, D, IN_COLS), D ** -0.5),
        'diff_lambda': nrm((L, 4, DIFF_HD), 0.1),
        'diff_subln': gain((L, 2 * DIFF_HD)),
        'ml_conv_w': nrm((L, CONV_K, ML_QK), CONV_K ** -0.5),
        'ml_conv_b': nrm((L, ML_QK), 0.02),
        'ml_gate_b': jnp.stack([zb, fb, zb, fb])[None] + nrm((L, 4, ML_H), 0.1),
        'ml_norm': gain((L, ML_V)),
        'mla_q_norm': gain((L, MLA_Q_RANK)),
        'mla_kv_norm': gain((L, MLA_KV_RANK)),
        'mla_w_uq': nrm((L, MLA_Q_RANK, MLA_H * (MLA_NOPE + MLA_ROPE)), MLA_Q_RANK ** -0.5),
        'mla_w_ukv': nrm((L, MLA_KV_RANK, MLA_H * (MLA_NOPE + MLA_V)), MLA_KV_RANK ** -0.5),
        'ssd_conv_w': nrm((L, CONV_K, SSD_XBC), CONV_K ** -0.5),
        'ssd_conv_b': nrm((L, SSD_XBC), 0.02),
        'ssd_dt_bias': dt0 + jnp.log(-jnp.expm1(-dt0)),
        'ssd_a_log': jnp.log(jax.random.uniform(next(ks), (L, 2, SSD_H), f32, 1.0, 16.0)),
        'ssd_d': gain((L, SSD_H)),
        'ssd_norm': gain((L, D_INNER)),
        'w_gate': nrm((L, N_BRANCH, D, D), D ** -0.5),
        'b_gate': nrm((L, N_BRANCH, D), 0.01),
        'w_branch': nrm((L, N_BRANCH, BRANCH_W, D), BRANCH_W ** -0.5),
        'w_o': nrm((L, D, D), DN_BETA * D ** -0.5),
        'ln1_g': gain((L, D)),
        'ln1_b': nrm((L, D), 0.01),
        'w_up': nrm((L, D, D_FF), D ** -0.5),
        'b_up': nrm((L, D_FF), 0.01),
        'w_down': nrm((L, D_FF, D), DN_BETA * D_FF ** -0.5),
        'b_down': nrm((L, D), 0.01),
        'ln2_g': gain((L, D)),
        'ln2_b': nrm((L, D), 0.01),
    }


def reference(x, c, ctx, c_ctx, w_mod, b_mod, w_in, diff_lambda, diff_subln, ml_conv_w, ml_conv_b,
              ml_gate_b, ml_norm, mla_q_norm, mla_kv_norm, mla_w_uq, mla_w_ukv, ssd_conv_w, ssd_conv_b,
              ssd_dt_bias, ssd_a_log, ssd_d, ssd_norm, w_gate, b_gate, w_branch, w_o, ln1_g, ln1_b,
              w_up, b_up, w_down, b_down, ln2_g, ln2_b):
    t = x.shape[1]
    rows = t // GRID_W
    row = jnp.repeat(jnp.arange(rows, dtype=jnp.int32), GRID_W)
    col = jnp.arange(t, dtype=jnp.int32) % GRID_W
    split_at = [int(s) for s in np.cumsum(IN_SPLITS)[:-1]]
    xc = ctx
    for l in range(DEPTH):
        ctx_out = l < DEPTH - 1
        lam_init = 0.8 - 0.6 * math.exp(-0.3 * l)
        mod_l = jax.nn.silu(c) @ w_mod[l] + b_mod[l]
        mod_c = jax.nn.silu(c_ctx) @ w_mod[l] + b_mod[l]
        sh_a, sc_a, g_a, sh_m, sc_m, g_m = jnp.split(mod_l[:, None, :], 6, axis=-1)
        csh_a, csc_a, cg_a, csh_m, csc_m, cg_m = jnp.split(mod_c, 6, axis=-1)

        h_l = modulate(x, sh_a, sc_a)
        h_c = modulate(xc, csh_a, csc_a)
        p_l = jnp.split(h_l @ w_in[l], split_at, axis=-1)
        p_c = jnp.split(h_c @ w_in[l], split_at, axis=-1)
        a_l, a_c = diff_attention(tuple(p_l[0:3]), tuple(p_c[0:3]), diff_lambda[l], diff_subln[l],
                                  lam_init, row, col, ctx_out)
        b_l, b_c = mlstm_mixer(tuple(p_l[3:7]), tuple(p_c[3:7]), ml_conv_w[l], ml_conv_b[l],
                               ml_gate_b[l], ml_norm[l], ctx_out)
        m_l, m_c = mla_mixer(tuple(p_l[7:10]), tuple(p_c[7:10]), mla_q_norm[l], mla_kv_norm[l],
                             mla_w_uq[l], mla_w_ukv[l], row, col, ctx_out)
        s_l, s_c = ssd_mixer(tuple(p_l[10:13]), tuple(p_c[10:13]), ssd_conv_w[l], ssd_conv_b[l],
                             ssd_dt_bias[l], ssd_a_log[l], ssd_d[l], ssd_norm[l], ctx_out)
        y_l = gated_merge(h_l, (a_l, b_l, m_l, s_l), w_gate[l], b_gate[l], w_branch[l], w_o[l])
        x = layer_norm(DN_ALPHA * x + g_a * y_l, ln1_g[l], ln1_b[l])
        f_l = sq_relu_mlp(modulate(x, sh_m, sc_m), w_up[l], b_up[l], w_down[l], b_down[l])
        x = layer_norm(DN_ALPHA * x + g_m * f_l, ln2_g[l], ln2_b[l])

        if ctx_out:
            y_c = gated_merge(h_c, (a_c, b_c, m_c, s_c), w_gate[l], b_gate[l], w_branch[l], w_o[l])
            xc = layer_norm(DN_ALPHA * xc + cg_a * y_c, ln1_g[l], ln1_b[l])
            f_c = sq_relu_mlp(modulate(xc, csh_m, csc_m), w_up[l], b_up[l], w_down[l], b_down[l])
            xc = layer_norm(DN_ALPHA * xc + cg_m * f_c, ln2_g[l], ln2_b[l])
    return x
```

```python
import functools
import math

import jax
import jax.numpy as jnp
from jax import lax
from jax.experimental import pallas as pl
from jax.experimental.pallas import tpu as pltpu

F32 = jnp.float32
BF16 = jnp.bfloat16

GRID_W = 64
ROPE_BASE = 10000.0
CONV_K = 5
EPS = 1e-6
LANES = 128
CONV_HALO = 8

DIFF_H, DIFF_HD = 4, 64
ML_H, ML_DK, ML_DV = 4, 64, 128
MLA_H, MLA_Q_RANK, MLA_KV_RANK, MLA_NOPE, MLA_ROPE, MLA_V = 4, 384, 256, 64, 32, 128
SSD_H, SSD_P, SSD_G, SSD_N = 8, 64, 2, 64
D_INNER = SSD_H * SSD_P

C_QD, C_KD, C_VD, C_MLQK, C_MLV, C_MLO, C_Z, C_XBC, C_CQ, C_CKV, C_SMALL, C_END = (
    0, 512, 1024, 1536, 2048, 2560, 3072, 3584, 4352, 4736, 4992, 5120)
SM_GATE, SM_DT, SM_KR = 0, 16, 64

VMEM_LIMIT = 60 * 1024 * 1024


def _cparams(sem):
    return pltpu.CompilerParams(dimension_semantics=sem, vmem_limit_bytes=VMEM_LIMIT)


def _ln(x):
    mu = jnp.mean(x, -1, keepdims=True)
    xc = x - mu
    var = jnp.mean(xc * xc, -1, keepdims=True)
    return xc * lax.rsqrt(var + EPS)


def _sigmoid(x):
    return 1.0 / (1.0 + jnp.exp(-x))


def _softplus(x):
    return jnp.maximum(x, 0.0) + jnp.log1p(jnp.exp(-jnp.abs(x)))


def _bdot(a, b):
    return jnp.dot(a.astype(BF16), b.astype(BF16), preferred_element_type=F32)


def _hdot(a, b):
    return jnp.dot(a, b, precision=lax.Precision.HIGHEST, preferred_element_type=F32)


def _mod_kernel(c_ref, w_ref, b_ref, o_ref):
    c = c_ref[...]
    o_ref[...] = _hdot(c * _sigmoid(c), w_ref[...]) + b_ref[...]


def _modulation(cvec, w_mod, b_mod):
    depth, d, n = w_mod.shape
    tn = 1536
    return pl.pallas_call(
        _mod_kernel,
        grid=(depth, n // tn),
        in_specs=[pl.BlockSpec((8, d), lambda l, j: (0, 0)),
                  pl.BlockSpec((None, d, tn), lambda l, j: (l, 0, j)),
                  pl.BlockSpec((None, 1, tn), lambda l, j: (l, 0, j))],
        out_specs=pl.BlockSpec((None, 8, tn), lambda l, j: (l, 0, j)),
        out_shape=jax.ShapeDtypeStruct((depth, 8, n), F32),
        compiler_params=_cparams(("arbitrary", "arbitrary")),
        name="modulation",
    )(cvec, w_mod, b_mod.reshape(depth, 1, n))


def _rope(x, cos, sin_next, sin_prev, partner):
    return (x * cos + pltpu.roll(x, LANES - partner, 1) * sin_next
            + pltpu.roll(x, partner, 1) * sin_prev)


def _rms(x, w):
    return x * lax.rsqrt(jnp.mean(x * x, -1, keepdims=True) + EPS) * w


def _inproj_kernel(x_ref, mod_ref, win_ref, rope_ref, qn_ref, kvn_ref, wuq_ref, wuk_ref, wuv_ref,
                   h_ref, qd_ref, kd_ref, vd_ref, mlqk_ref, mlv_ref, mlo_ref, z_ref, xbc_ref,
                   small_ref, qm_ref, km_ref, vm_ref):
    x = x_ref[...]
    h = _ln(x) * (1.0 + mod_ref[1:2, :]) + mod_ref[0:1, :]
    hb = h.astype(BF16)
    h_ref[...] = hb

    def proj(lo, hi):
        return jnp.dot(hb, win_ref[:, lo:hi], preferred_element_type=F32)

    cos_d, sn_d, sp_d = rope_ref[:, 0:128], rope_ref[:, 128:256], rope_ref[:, 256:384]
    cos_m, sn_m, sp_m = rope_ref[:, 384:512], rope_ref[:, 512:640], rope_ref[:, 640:768]

    qd = proj(C_QD, C_KD)
    kd = proj(C_KD, C_VD)
    for hh in range(DIFF_H):
        sl = slice(hh * LANES, (hh + 1) * LANES)
        qd_ref[:, sl] = (_rope(qd[:, sl], cos_d, sn_d, sp_d, 16) * (DIFF_HD ** -0.5)).astype(BF16)
        kd_ref[:, sl] = _rope(kd[:, sl], cos_d, sn_d, sp_d, 16).astype(BF16)
    vd_ref[...] = proj(C_VD, C_MLQK).astype(BF16)
    mlqk_ref[...] = proj(C_MLQK, C_MLV)
    mlv_ref[...] = proj(C_MLV, C_MLO).astype(BF16)
    mlo_ref[...] = proj(C_MLO, C_Z).astype(BF16)
    z_ref[...] = proj(C_Z, C_XBC).astype(BF16)
    xbc_ref[...] = proj(C_XBC, C_CQ)
    small = proj(C_SMALL, C_END)
    small_ref[...] = small

    cq = _rms(proj(C_CQ, C_CKV), qn_ref[...])
    ckv = _rms(proj(C_CKV, C_SMALL), kvn_ref[...])
    qm = _bdot(cq, wuq_ref[...])
    km = _bdot(ckv, wuk_ref[...])
    vm_ref[...] = _bdot(ckv, wuv_ref[...]).astype(BF16)
    lane = lax.broadcasted_iota(jnp.int32, small.shape, 1)
    kr = jnp.where((lane >= SM_KR) & (lane < SM_KR + MLA_ROPE), small, 0.0)
    kr = _rope(kr, cos_m, sn_m, sp_m, 8)
    scale = (MLA_NOPE + MLA_ROPE) ** -0.5
    for hh in range(MLA_H):
        sl = slice(hh * LANES, (hh + 1) * LANES)
        qm_ref[:, sl] = (_rope(qm[:, sl], cos_m, sn_m, sp_m, 8) * scale).astype(BF16)
        km_ref[:, sl] = (km[:, sl] + kr).astype(BF16)


def _inproj(xall, mod, nct, w_in, rope, qn, kvn, wuq, wuk, wuv, tm):
    s, d = xall.shape
    n = s // tm
    row = lambda w: pl.BlockSpec((tm, w), lambda i: (i, 0))
    full = lambda a: pl.BlockSpec(a.shape, lambda i: (0,) * a.ndim)
    outs = [(d, BF16), (512, BF16), (512, BF16), (512, BF16), (512, F32), (512, BF16), (512, BF16),
            (512, BF16), (768, F32), (128, F32), (512, BF16), (512, BF16), (512, BF16)]
    return pl.pallas_call(
        _inproj_kernel,
        grid=(n,),
        in_specs=[row(d),
                  pl.BlockSpec((None, 8, d), lambda i: (jnp.where(i < nct, 0, 1), 0, 0)),
                  full(w_in), row(768), full(qn), full(kvn), full(wuq), full(wuk), full(wuv)],
        out_specs=[row(w) for w, _ in outs],
        out_shape=[jax.ShapeDtypeStruct((s, w), dt) for w, dt in outs],
        compiler_params=_cparams(("arbitrary",)),
        name="inproj",
    )(xall, mod, w_in, rope, qn, kvn, wuq, wuk, wuv)


def _conv_kernel(cur_ref, prev_ref, next_ref, w_ref, b_ref, *rest, splits, scales, nct, n):
    out_refs, ext_ref = rest[:-1], rest[-1]
    i = pl.program_id(0)
    tm = cur_ref.shape[0]
    has_prev = jnp.logical_and(i != 0, i != nct).astype(F32)
    has_next = jnp.logical_and(i != nct - 1, i != n - 1).astype(F32)
    ext_ref[0:CONV_HALO, :] = prev_ref[...] * has_prev
    ext_ref[CONV_HALO:CONV_HALO + tm, :] = cur_ref[...]
    ext_ref[CONV_HALO + tm:, :] = next_ref[...] * has_next
    acc = b_ref[...] + w_ref[0:1, :] * ext_ref[pl.ds(CONV_HALO - 2, tm), :]
    for k in range(1, CONV_K):
        acc = acc + w_ref[k:k + 1, :] * ext_ref[pl.ds(CONV_HALO - 2 + k, tm), :]
    y = acc * _sigmoid(acc)
    lo = 0
    for ref, wd, sc in zip(out_refs, splits, scales):
        ref[...] = (y[:, lo:lo + wd] * sc).astype(ref.dtype)
        lo += wd


def _conv_silu(xin, w, b, nct, tm, splits, scales):
    s, c = xin.shape
    n = s // tm
    r = tm // CONV_HALO
    wpad = jnp.zeros((8, c), F32).at[:CONV_K].set(w)
    return pl.pallas_call(
        functools.partial(_conv_kernel, splits=splits, scales=scales, nct=nct, n=n),
        grid=(n,),
        in_specs=[pl.BlockSpec((tm, c), lambda i: (i, 0)),
                  pl.BlockSpec((CONV_HALO, c), lambda i: (jnp.maximum(i * r - 1, 0), 0)),
                  pl.BlockSpec((CONV_HALO, c), lambda i: (jnp.minimum((i + 1) * r, n * r - 1), 0)),
                  pl.BlockSpec((8, c), lambda i: (0, 0)),
                  pl.BlockSpec((1, c), lambda i: (0, 0))],
        out_specs=[pl.BlockSpec((tm, wd), lambda i: (i, 0)) for wd in splits],
        out_shape=[jax.ShapeDtypeStruct((s, wd), BF16) for wd in splits],
        scratch_shapes=[pltpu.VMEM((tm + 2 * CONV_HALO, c), F32)],
        compiler_params=_cparams(("arbitrary",)),
        name="conv_silu",
    )(xin, xin, xin, wpad, b.reshape(1, c))


def _flash_kernel(q_ref, k_ref, v_ref, lam_ref, sub_ref, o_ref, *, diff, tk, nkv, lam_init):
    q = q_ref[...]
    tq = q.shape[0]
    if diff:
        lane = lax.broadcasted_iota(jnp.int32, q.shape, 1)
        zero = jnp.zeros_like(q)
        q = jnp.concatenate([jnp.where(lane < DIFF_HD, q, zero), jnp.where(lane >= DIFF_HD, q, zero)], 0)
    rows = q.shape[0]

    def body(j, carry):
        m, l, acc = carry
        off = pl.multiple_of(j * tk, tk)
        ks = k_ref[pl.ds(off, tk), :]
        vs = v_ref[pl.ds(off, tk), :]
        s = lax.dot_general(q, ks, (((1,), (1,)), ((), ())), preferred_element_type=F32)
        m_new = jnp.maximum(m, jnp.max(s, -1, keepdims=True))
        p = jnp.exp(s - m_new)
        alpha = jnp.exp(m - m_new)
        l = alpha * l + jnp.sum(p, -1, keepdims=True)
        acc = alpha * acc + jnp.dot(p.astype(BF16), vs, preferred_element_type=F32)
        return m_new, l, acc

    init = (jnp.full((rows, 1), -jnp.inf, F32), jnp.zeros((rows, 1), F32), jnp.zeros((rows, LANES), F32))
    _, l, acc = lax.fori_loop(0, nkv, body, init)
    o = acc / l
    if diff:
        lam = lam_ref[...]
        lam_full = (jnp.exp(jnp.sum(lam[0:1, :] * lam[1:2, :], -1, keepdims=True))
                    - jnp.exp(jnp.sum(lam[2:3, :] * lam[3:4, :], -1, keepdims=True)) + lam_init)
        o = o[:tq] - lam_full * o[tq:]
        o = _rms(o, sub_ref[...]) * (1.0 - lam_init)
    o_ref[...] = o.astype(o_ref.dtype)


def _flash(q, k, v, lam, subln, *, diff, q_tile0, nq, kv_len, tq, tk, lam_init=0.0):
    heads = q.shape[1] // LANES
    return pl.pallas_call(
        functools.partial(_flash_kernel, diff=diff, tk=tk, nkv=kv_len // tk, lam_init=lam_init),
        grid=(heads, nq),
        in_specs=[pl.BlockSpec((tq, LANES), lambda h, i: (i + q_tile0, h)),
                  pl.BlockSpec((kv_len, LANES), lambda h, i: (0, h)),
                  pl.BlockSpec((kv_len, LANES), lambda h, i: (0, h)),
                  pl.BlockSpec(lam.shape, lambda h, i: (0, 0)),
                  pl.BlockSpec(subln.shape, lambda h, i: (0, 0))],
        out_specs=pl.BlockSpec((tq, LANES), lambda h, i: (i, h)),
        out_shape=jax.ShapeDtypeStruct((nq * tq, heads * LANES), BF16),
        compiler_params=_cparams(("arbitrary", "arbitrary")),
        name="flash_diff" if diff else "flash_mla",
    )(q, k, v, lam, subln)


def _attention(q, k, v, lam, subln, *, diff, ctx_len, ctx_out, tq, tk, lam_init=0.0):
    s = q.shape[0]
    nctq = ctx_len // tq
    lat = _flash(q, k, v, lam, subln, diff=diff, q_tile0=nctq, nq=s // tq - nctq, kv_len=s, tq=tq, tk=tk,
                 lam_init=lam_init)
    if not ctx_out:
        return lat, None
    ctx = _flash(q, k, v, lam, subln, diff=diff, q_tile0=0, nq=nctq, kv_len=ctx_len, tq=tq,
                 tk=math.gcd(tk, ctx_len), lam_init=lam_init)
    return lat, ctx


def _tri(n, upper):
    r = lax.broadcasted_iota(jnp.int32, (n, n), 0)
    c = lax.broadcasted_iota(jnp.int32, (n, n), 1)
    return (c >= r) if upper else (c <= r)


def _mlstm_dir(bwd, q_ref, kt_ref, v_ref, sm_ref, smt_ref, gb_ref, gbt_ref, o_ref, c_ref, m_ref, first):
    cl = q_ref.shape[0]

    @pl.when(first)
    def _():
        c_ref[...] = jnp.zeros_like(c_ref)
        m_ref[...] = jnp.zeros_like(m_ref)

    allowed = _tri(cl, bwd)
    cum = allowed.astype(F32)
    pre = sm_ref[...] + gb_ref[...]
    pre_t = smt_ref[...] + gbt_ref[...]
    lf = -_softplus(-pre)
    lf_t = -_softplus(-pre_t)
    g_col = _hdot(cum, lf)
    g_row = _hdot(lf_t, cum.T)
    q = q_ref[...]
    kt = kt_ref[...]
    lane_q = lax.broadcasted_iota(jnp.int32, q.shape, 1)
    ones = jnp.ones((cl, LANES), BF16)
    end = 0 if bwd else cl - 1
    for hh in range(ML_H):
        ci, cf = (2 * bwd) * ML_H + hh, (2 * bwd + 1) * ML_H + hh
        g_c, g_r = g_col[:, cf:cf + 1], g_row[cf:cf + 1, :]
        i_c, i_r = pre[:, ci:ci + 1], pre_t[ci:ci + 1, :]
        m_prev = m_ref[hh:hh + 1, 0:1]
        qh = jnp.where((lane_q >= hh * ML_DK) & (lane_q < (hh + 1) * ML_DK), q, jnp.zeros_like(q))
        vh = jnp.concatenate([v_ref[:, hh * ML_DV:(hh + 1) * ML_DV], ones], 1)
        logw = jnp.where(allowed, g_c - g_r + i_r, -jnp.inf)
        log_inter = g_c + m_prev
        m_j = jnp.maximum(log_inter, jnp.max(logw, -1, keepdims=True))
        w = jnp.exp(logw - m_j) * jnp.dot(qh, kt, preferred_element_type=F32)
        a_inter = jnp.exp(log_inter - m_j)
        cstate = c_ref[...]
        nd = _bdot(w, vh) + a_inter * _bdot(qh, cstate)
        num, den = nd[:, :ML_DV], nd[:, ML_DV:ML_DV + 1]
        o_ref[:, hh * ML_DV:(hh + 1) * ML_DV] = num / jnp.maximum(jnp.abs(den), jnp.exp(-m_j))
        g_end = g_c[end:end + 1, :]
        log_s = g_end - g_c + i_c
        m_new = jnp.maximum(g_end + m_prev, jnp.max(log_s, 0, keepdims=True))
        ws = jnp.exp(log_s - m_new)
        decay = jnp.exp(g_end + m_prev - m_new)
        rows = slice(hh * ML_DK, (hh + 1) * ML_DK)
        upd = jnp.dot(kt[rows, :], (ws * vh.astype(F32)).astype(BF16), preferred_element_type=F32)
        c_ref[rows, :] = decay * cstate[rows, :] + upd
        m_ref[hh:hh + 1, :] = jnp.broadcast_to(m_new, (1, LANES))


def _mlstm_kernel(qf, ktf, vf, smf, smtf, qb, ktb, vb, smb, smtb, gb_ref, gbt_ref, of_ref, ob_ref,
                  cf_ref, mf_ref, cb_ref, mb_ref):
    first = pl.program_id(0) == 0
    _mlstm_dir(0, qf, ktf, vf, smf, smtf, gb_ref, gbt_ref, of_ref, cf_ref, mf_ref, first)
    _mlstm_dir(1, qb, ktb, vb, smb, smtb, gb_ref, gbt_ref, ob_ref, cb_ref, mb_ref, first)


def _scan_orders(nc, ncc):
    fwd = lambda c: c
    bwd = lambda c: jnp.where(c < ncc, ncc - 1 - c, nc - 1 - (c - ncc))
    return fwd, bwd


def _mlstm_scan(q, kt, v, small, small_t, gate_b, ctx_len, cl):
    s = q.shape[0]
    nc, ncc = s // cl, ctx_len // cl
    gb = jnp.zeros((1, LANES), F32).at[0, SM_GATE:SM_GATE + 4 * ML_H].set(gate_b.reshape(-1))

    def specs(order):
        return [pl.BlockSpec((cl, ML_H * ML_DK), lambda c: (order(c), 0)),
                pl.BlockSpec((ML_H * ML_DK, cl), lambda c: (0, order(c))),
                pl.BlockSpec((cl, ML_H * ML_DV), lambda c: (order(c), 0)),
                pl.BlockSpec((cl, LANES), lambda c: (order(c), 0)),
                pl.BlockSpec((LANES, cl), lambda c: (0, order(c)))]

    fwd, bwd = _scan_orders(nc, ncc)
    const = lambda a: pl.BlockSpec(a.shape, lambda c: (0, 0))
    gbt = gb.reshape(LANES, 1)
    state = [pltpu.VMEM((ML_H * ML_DK, 2 * ML_DV), F32), pltpu.VMEM((8, LANES), F32)]
    return pl.pallas_call(
        _mlstm_kernel,
        grid=(nc,),
        in_specs=specs(fwd) + specs(bwd) + [const(gb), const(gbt)],
        out_specs=[pl.BlockSpec((cl, ML_H * ML_DV), lambda c: (fwd(c), 0)),
                   pl.BlockSpec((cl, ML_H * ML_DV), lambda c: (bwd(c), 0))],
        out_shape=[jax.ShapeDtypeStruct((s, ML_H * ML_DV), F32)] * 2,
        scratch_shapes=state + state,
        compiler_params=_cparams(("arbitrary",)),
        name="mlstm_scan",
    )(q, kt, v, small, small_t, q, kt, v, small, small_t, gb, gbt)


def _ssd_dir(bwd, xs_ref, bt_ref, cm_ref, sm_ref, smt_ref, db_ref, dbt_ref, al_ref, alt_ref, ex_ref,
             o_ref, st_ref, first):
    cl = xs_ref.shape[0]

    @pl.when(first)
    def _():
        st_ref[...] = jnp.zeros_like(st_ref)

    allowed = _tri(cl, bwd)
    cum = allowed.astype(F32)
    dt = _softplus(sm_ref[...] + db_ref[...])
    dt_t = _softplus(smt_ref[...] + dbt_ref[...])
    a = dt * (-jnp.exp(al_ref[...]))
    a_t = dt_t * (-jnp.exp(alt_ref[...]))
    s_col = _hdot(cum, a)
    s_row = _hdot(a_t, cum.T)
    end = 0 if bwd else cl - 1
    s_end = s_col[end:end + 1, :]
    stack = jnp.concatenate([jnp.exp(s_col), dt, dt * jnp.exp(s_end - s_col), jnp.exp(jnp.broadcast_to(s_end, (8, LANES)))], 0)
    wide = _hdot(stack, ex_ref[bwd])
    es_w, dt_w, wt_w, dec_w = wide[:cl], wide[cl:2 * cl], wide[2 * cl:3 * cl], wide[3 * cl:3 * cl + 1]
    xs = xs_ref[...].astype(F32)
    xdt = (xs * dt_w).astype(BF16)
    cm = cm_ref[...]
    bt = bt_ref[...]
    lane_c = lax.broadcasted_iota(jnp.int32, cm.shape, 1)
    lane_x = lax.broadcasted_iota(jnp.int32, (cl, LANES), 1)
    state = st_ref[...]
    y = es_w * _bdot(cm, state)
    per_pair = SSD_H // (D_INNER // LANES)
    for blk in range(D_INNER // LANES):
        grp = (blk * per_pair) // (SSD_H // SSD_G)
        cg = jnp.where((lane_c >= grp * SSD_N) & (lane_c < (grp + 1) * SSD_N), cm, jnp.zeros_like(cm))
        cb = jnp.dot(cg, bt, preferred_element_type=F32)
        xblk = xdt[:, blk * LANES:(blk + 1) * LANES]
        res = []
        for sub in range(per_pair):
            hh = blk * per_pair + sub
            col = SM_DT + bwd * SSD_H + hh
            decay = jnp.exp(jnp.where(allowed, s_col[:, col:col + 1] - s_row[col:col + 1, :], -jnp.inf))
            res.append(_bdot(decay * cb, xblk))
        intra = jnp.where(lane_x < SSD_P, res[0], res[1])
        o_ref[:, blk * LANES:(blk + 1) * LANES] = y[:, blk * LANES:(blk + 1) * LANES] + intra
    upd = jnp.dot(bt, (xs * wt_w).astype(BF16), preferred_element_type=F32)
    r = lax.broadcasted_iota(jnp.int32, upd.shape, 0) // SSD_N
    c = lax.broadcasted_iota(jnp.int32, upd.shape, 1) // (SSD_P * (SSD_H // SSD_G))
    st_ref[...] = jnp.where(r == c, dec_w * state + upd, 0.0)


def _ssd_kernel(xf, btf, cf, smf, smtf, xb, btb, cb, smb, smtb, db_ref, dbt_ref, al_ref, alt_ref, ex_ref,
                of_ref, ob_ref, sf_ref, sb_ref):
    first = pl.program_id(0) == 0
    _ssd_dir(0, xf, btf, cf, smf, smtf, db_ref, dbt_ref, al_ref, alt_ref, ex_ref, of_ref, sf_ref, first)
    _ssd_dir(1, xb, btb, cb, smb, smtb, db_ref, dbt_ref, al_ref, alt_ref, ex_ref, ob_ref, sb_ref, first)


def _ssd_scan(xs, bt, cm, small, small_t, dt_bias, a_log, ctx_len, cl):
    s = xs.shape[0]
    nc, ncc = s // cl, ctx_len // cl
    db = jnp.zeros((1, LANES), F32).at[0, SM_DT:SM_DT + 2 * SSD_H].set(dt_bias.reshape(-1))
    al = jnp.zeros((1, LANES), F32).at[0, SM_DT:SM_DT + 2 * SSD_H].set(a_log.reshape(-1))
    src = jnp.arange(LANES)[None, :, None]
    dst_head = (jnp.arange(D_INNER) // SSD_P)[None, None, :]
    dirs = jnp.arange(2)[:, None, None]
    expand = (src == SM_DT + dirs * SSD_H + dst_head).astype(F32)

    def specs(order):
        return [pl.BlockSpec((cl, D_INNER), lambda c: (order(c), 0)),
                pl.BlockSpec((SSD_G * SSD_N, cl), lambda c: (0, order(c))),
                pl.BlockSpec((cl, SSD_G * SSD_N), lambda c: (order(c), 0)),
                pl.BlockSpec((cl, LANES), lambda c: (order(c), 0)),
                pl.BlockSpec((LANES, cl), lambda c: (0, order(c)))]

    fwd, bwd = _scan_orders(nc, ncc)
    const = lambda a: pl.BlockSpec(a.shape, lambda c: (0,) * a.ndim)
    dbt, alt = db.reshape(LANES, 1), al.reshape(LANES, 1)
    state = pltpu.VMEM((SSD_G * SSD_N, D_INNER), F32)
    return pl.pallas_call(
        _ssd_kernel,
        grid=(nc,),
        in_specs=specs(fwd) + specs(bwd) + [const(db), const(dbt), const(al), const(alt), const(expand)],
        out_specs=[pl.BlockSpec((cl, D_INNER), lambda c: (fwd(c), 0)),
                   pl.BlockSpec((cl, D_INNER), lambda c: (bwd(c), 0))],
        out_shape=[jax.ShapeDtypeStruct((s, D_INNER), F32)] * 2,
        scratch_shapes=[state, state],
        compiler_params=_cparams(("arbitrary",)),
        name="ssd_scan",
    )(xs, bt, cm, small, small_t, xs, bt, cm, small, small_t, db, dbt, al, alt, expand)


def _merge_kernel(x_ref, mod_ref, h_ref, a_ref, hf_ref, hb_ref, mlo_ref, mln_ref, m_ref, yf_ref, yb_ref,
                  xs_ref, z_ref, dsk_ref, sdn_ref, wg_ref, bg_ref, wb_ref, wo_ref, g_ref, b_ref, o_ref,
                  *, alpha):
    hb16 = h_ref[...]
    hm = hf_ref[...] + hb_ref[...]
    og = _sigmoid(mlo_ref[...].astype(F32))
    b_parts = []
    for hh in range(ML_H):
        sl = slice(hh * ML_DV, (hh + 1) * ML_DV)
        b_parts.append(_rms(hm[:, sl], mln_ref[:, sl]) * og[:, sl])
    b_br = jnp.concatenate(b_parts, 1)
    z = z_ref[...].astype(F32)
    ys = (yf_ref[...] + yb_ref[...] + dsk_ref[...] * xs_ref[...].astype(F32)) * (z * _sigmoid(z))
    gw = D_INNER // SSD_G
    s_br = jnp.concatenate([_rms(ys[:, g * gw:(g + 1) * gw], sdn_ref[:, g * gw:(g + 1) * gw])
                            for g in range(SSD_G)], 1)
    branches = (a_ref[...], b_br, m_ref[...], s_br)
    y = None
    for k, br in enumerate(branches):
        gate = _sigmoid(jnp.dot(hb16, wg_ref[k], preferred_element_type=F32) + bg_ref[k:k + 1, :])
        term = gate * _bdot(br, wb_ref[k])
        y = term if y is None else y + term
    y = _bdot(y, wo_ref[...])
    x1 = alpha * x_ref[...] + mod_ref[2:3, :] * y
    o_ref[...] = _ln(x1) * g_ref[...] + b_ref[...]


def _merge(xall, mod, nct, tile0, n_tiles, h, a, hf, hb, mlo, mln, m, yf, yb, xs, z, dsk, sdn, wg, bg, wb, wo,
           g, b, tm, alpha):
    s, d = xall.shape
    row = lambda w: pl.BlockSpec((tm, w), lambda i: (i + tile0, 0))
    own = lambda w: pl.BlockSpec((tm, w), lambda i: (i, 0))
    full = lambda arr: pl.BlockSpec(arr.shape, lambda i: (0,) * arr.ndim)
    return pl.pallas_call(
        functools.partial(_merge_kernel, alpha=alpha),
        grid=(n_tiles,),
        in_specs=[row(d), pl.BlockSpec((None, 8, d), lambda i: (jnp.where(i + tile0 < nct, 0, 1), 0, 0)),
                  row(d), own(512), row(512), row(512), row(512), full(mln), own(512), row(512), row(512),
                  row(512), row(512), full(dsk), full(sdn), full(wg), full(bg), full(wb), full(wo),
                  full(g), full(b)],
        out_specs=own(d),
        out_shape=jax.ShapeDtypeStruct((n_tiles * tm, d), F32),
        compiler_params=_cparams(("arbitrary",)),
        name="merge_ln1",
    )(xall, mod, h, a, hf, hb, mlo, mln, m, yf, yb, xs, z, dsk, sdn, wg, bg, wb, wo, g, b)


def _mlp_kernel(x_ref, mod_ref, wu_ref, bu_ref, wd_ref, bd_ref, g_ref, b_ref, o_ref, *, alpha):
    x = x_ref[...]
    hm = _ln(x) * (1.0 + mod_ref[4:5, :]) + mod_ref[3:4, :]
    u = jnp.maximum(_bdot(hm, wu_ref[...]) + bu_ref[...], 0.0)
    f = _bdot(u * u, wd_ref[...]) + bd_ref[...]
    x2 = alpha * x + mod_ref[5:6, :] * f
    o_ref[...] = _ln(x2) * g_ref[...] + b_ref[...]


def _mlp(x1, mod, nct, tile0, wu, bu, wd, bd, g, b, tm, alpha):
    s, d = x1.shape
    n_tiles = s // tm
    full = lambda arr: pl.BlockSpec(arr.shape, lambda i: (0,) * arr.ndim)
    return pl.pallas_call(
        functools.partial(_mlp_kernel, alpha=alpha),
        grid=(n_tiles,),
        in_specs=[pl.BlockSpec((tm, d), lambda i: (i, 0)),
                  pl.BlockSpec((None, 8, d), lambda i: (jnp.where(i + tile0 < nct, 0, 1), 0, 0)),
                  full(wu), full(bu), full(wd), full(bd), full(g), full(b)],
        out_specs=pl.BlockSpec((tm, d), lambda i: (i, 0)),
        out_shape=jax.ShapeDtypeStruct((n_tiles * tm, d), F32),
        compiler_params=_cparams(("arbitrary",)),
        name="mlp_ln2",
    )(x1, mod, wu, bu, wd, bd, g, b)


def _axis_tables(pos, dsub):
    half = dsub // 2
    inv = ROPE_BASE ** (-jnp.arange(half, dtype=F32) * 2.0 / dsub)
    ang = pos[:, None] * inv[None, :]
    c, s = jnp.cos(ang), jnp.sin(ang)
    z = jnp.zeros_like(s)
    return jnp.concatenate([c, c], -1), jnp.concatenate([-s, z], -1), jnp.concatenate([z, s], -1)


def _rope_tables(t, ctx_len):
    pos = jnp.arange(t, dtype=jnp.int32)
    row, col = (pos // GRID_W).astype(F32), (pos % GRID_W).astype(F32)

    def two_axis(dsub):
        r, c = _axis_tables(row, dsub), _axis_tables(col, dsub)
        return [jnp.concatenate([a, b], -1) for a, b in zip(r, c)]

    cd, nd, pd = [jnp.tile(a, (1, 2)) for a in two_axis(DIFF_HD // 2)]
    cm, nm, pm = two_axis(MLA_ROPE // 2)
    ones = lambda w: jnp.ones((t, w), F32)
    zeros = lambda w: jnp.zeros((t, w), F32)
    pad = LANES - MLA_NOPE - MLA_ROPE
    cm = jnp.concatenate([ones(MLA_NOPE), cm, ones(pad)], -1)
    nm = jnp.concatenate([zeros(MLA_NOPE), nm, zeros(pad)], -1)
    pm = jnp.concatenate([zeros(MLA_NOPE), pm, zeros(pad)], -1)
    lat = jnp.concatenate([cd, nd, pd, cm, nm, pm], -1)
    ident = jnp.concatenate([jnp.ones((ctx_len, LANES), F32), jnp.zeros((ctx_len, 2 * LANES), F32)] * 2, -1)
    return jnp.concatenate([ident, lat], 0)


def _relayout_w_in(w):
    d = w.shape[0]
    o = [0, 512, 1024, 1536, 2048, 2560, 3072, 3088, 3472, 3728, 3760, 4272, 5040, 5056]
    seg = lambda i: w[:, o[i]:o[i + 1]]
    qd, kd, vd, mlqk, mlv, mlo, mlg, cq, ckv, kr, z, xbc, dt = [seg(i) for i in range(13)]
    zc = lambda n: jnp.zeros((d, n), w.dtype)
    small = jnp.concatenate([mlg, dt, zc(SM_KR - 32), kr, zc(LANES - SM_KR - MLA_ROPE)], 1)
    return jnp.concatenate([qd, kd, vd, mlqk, mlv, mlo, z, xbc, cq, ckv, small], 1).astype(BF16)


def _pad_heads(w, heads, width):
    r = w.shape[0]
    w = w.reshape(r, heads, width)
    return jnp.pad(w, ((0, 0), (0, 0), (0, LANES - width))).reshape(r, heads * LANES)


def _pick_tile(n, pref):
    for t in pref:
        if n % t == 0:
            return t
    raise ValueError(f"no tile for {n}")


def kernel(x, c, ctx, c_ctx, w_mod, b_mod, w_in, diff_lambda, diff_subln, ml_conv_w, ml_conv_b, ml_gate_b, ml_norm, mla_q_norm, mla_kv_norm, mla_w_uq, mla_w_ukv, ssd_conv_w, ssd_conv_b, ssd_dt_bias, ssd_a_log, ssd_d, ssd_norm, w_gate, b_gate, w_branch, w_o, ln1_g, ln1_b, w_up, b_up, w_down, b_down, ln2_g, ln2_b):
    assert x.shape[0] == 1 and ctx.shape[0] == 1
    depth = w_in.shape[0]
    t, d = x.shape[1], x.shape[2]
    ctx_len = ctx.shape[1]
    s = ctx_len + t
    tm = _pick_tile(math.gcd(ctx_len, t), (256, 128))
    cl = 128
    tq_diff, tq_mla = tm, tm
    tk = _pick_tile(s, (1280, 640, 256, 128))
    assert ctx_len % cl == 0 and t % cl == 0 and t % GRID_W == 0
    nct, n = ctx_len // tm, s // tm
    alpha = (2 * depth) ** 0.25

    xall = jnp.concatenate([ctx[0], x[0]], 0)
    cvec = jnp.zeros((8, d), F32).at[0].set(c_ctx).at[1].set(c[0])
    mod = _modulation(cvec, w_mod, b_mod)
    mod = jnp.pad(mod[:, :2].reshape(depth, 2, 6, d), ((0, 0), (0, 0), (0, 2), (0, 0)))
    rope = _rope_tables(t, ctx_len)

    for l in range(depth):
        ctx_out = l < depth - 1
        lam_init = 0.8 - 0.6 * math.exp(-0.3 * l)
        wq = _pad_heads(mla_w_uq[l], MLA_H, MLA_NOPE + MLA_ROPE).astype(BF16)
        wkv = mla_w_ukv[l].reshape(MLA_KV_RANK, MLA_H, MLA_NOPE + MLA_V)
        wk = _pad_heads(wkv[:, :, :MLA_NOPE].reshape(MLA_KV_RANK, -1), MLA_H, MLA_NOPE).astype(BF16)
        wv = wkv[:, :, MLA_NOPE:].reshape(MLA_KV_RANK, -1).astype(BF16)
        (h, qd, kd, vd, mlqk, mlv, mlo, z, xbc, small, qm, km, vm) = _inproj(
            xall, mod[l], nct, _relayout_w_in(w_in[l]), rope, mla_q_norm[l][None], mla_kv_norm[l][None],
            wq, wk, wv, tm)

        a_lat, a_ctx = _attention(qd, kd, vd, diff_lambda[l], diff_subln[l][None], diff=True,
                                  ctx_len=ctx_len, ctx_out=ctx_out, tq=tq_diff, tk=tk, lam_init=lam_init)
        m_lat, m_ctx = _attention(qm, km, vm, diff_lambda[l], diff_subln[l][None], diff=False,
                                  ctx_len=ctx_len, ctx_out=ctx_out, tq=tq_mla, tk=tk)

        ml_q, ml_k = _conv_silu(mlqk, ml_conv_w[l], ml_conv_b[l], nct, tm,
                                (ML_H * ML_DK, ML_H * ML_DK), (1.0, ML_DK ** -0.5))
        xs, bm, cm = _conv_silu(xbc, ssd_conv_w[l], ssd_conv_b[l], nct, tm,
                                (D_INNER, SSD_G * SSD_N, SSD_G * SSD_N), (1.0, 1.0, 1.0))
        small_t = small.T
        hf, hb = _mlstm_scan(ml_q, ml_k.T, mlv, small, small_t, ml_gate_b[l], ctx_len, cl)
        yf, yb = _ssd_scan(xs, bm.T, cm, small, small_t, ssd_dt_bias[l], ssd_a_log[l], ctx_len, cl)

        if ctx_out:
            a_all = jnp.concatenate([a_ctx, a_lat], 0)
            m_all = jnp.concatenate([m_ctx, m_lat], 0)
            tile0, n_tiles = 0, n
        else:
            a_all, m_all = a_lat, m_lat
            tile0, n_tiles = nct, n - nct
        x1 = _merge(xall, mod[l], nct, tile0, n_tiles, h, a_all, hf, hb, mlo, ml_norm[l][None], m_all, yf, yb,
                    xs, z, jnp.repeat(ssd_d[l], SSD_P)[None], ssd_norm[l][None], w_gate[l].astype(BF16),
                    b_gate[l], w_branch[l].astype(BF16), w_o[l].astype(BF16), ln1_g[l][None], ln1_b[l][None],
                    tm, alpha)
        xall = _mlp(x1, mod[l], nct, tile0, w_up[l].astype(BF16), b_up[l][None],
                    w_down[l].astype(BF16), b_down[l][None], ln2_g[l][None], ln2_b[l][None], tm, alpha)
    return xall[None]
```

```python
import functools
import math

import jax
import jax.numpy as jnp
from jax import lax
from jax.experimental import pallas as pl
from jax.experimental.pallas import tpu as pltpu

F32 = jnp.float32
BF16 = jnp.bfloat16

GRID_W = 64
ROPE_BASE = 10000.0
CONV_K = 5
EPS = 1e-6
LOG2E = math.log2(math.e)
LANES = 128
CONV_HALO = 8

DIFF_H, DIFF_HD = 4, 64
ML_H, ML_DK, ML_DV = 4, 64, 128
MLA_H, MLA_Q_RANK, MLA_KV_RANK, MLA_NOPE, MLA_ROPE, MLA_V = 4, 384, 256, 64, 32, 128
SSD_H, SSD_P, SSD_G, SSD_N = 8, 64, 2, 64
D_INNER = SSD_H * SSD_P

C_QD, C_KD, C_VD, C_MLQK, C_MLV, C_MLO, C_Z, C_XBC, C_CQ, C_CKV, C_SMALL, C_END = (
    0, 512, 1024, 1536, 2048, 2560, 3072, 3584, 4352, 4736, 4992, 5120)
SM_GATE, SM_DT, SM_KR = 0, 16, 64

VMEM_LIMIT = 60 * 1024 * 1024


def _cparams(sem):
    return pltpu.CompilerParams(dimension_semantics=sem, vmem_limit_bytes=VMEM_LIMIT)


def _ln(x):
    mu = jnp.mean(x, -1, keepdims=True)
    xc = x - mu
    var = jnp.mean(xc * xc, -1, keepdims=True)
    return xc * lax.rsqrt(var + EPS)


def _sigmoid(x):
    return 1.0 / (1.0 + jnp.exp(-x))


def _softplus(x):
    return jnp.maximum(x, 0.0) + jnp.log1p(jnp.exp(-jnp.abs(x)))


def _bdot(a, b):
    return jnp.dot(a.astype(BF16), b.astype(BF16), preferred_element_type=F32)


def _hdot(a, b):
    return jnp.dot(a, b, precision=lax.Precision.HIGHEST, preferred_element_type=F32)


def _mod_kernel(c_ref, w_ref, b_ref, o_ref):
    c = c_ref[...]
    o_ref[...] = _hdot(c * _sigmoid(c), w_ref[...]) + b_ref[...]


def _modulation(cvec, w_mod, b_mod):
    depth, d, n = w_mod.shape
    tn = 1536
    return pl.pallas_call(
        _mod_kernel,
        grid=(depth, n // tn),
        in_specs=[pl.BlockSpec((8, d), lambda l, j: (0, 0)),
                  pl.BlockSpec((None, d, tn), lambda l, j: (l, 0, j)),
                  pl.BlockSpec((None, 1, tn), lambda l, j: (l, 0, j))],
        out_specs=pl.BlockSpec((None, 8, tn), lambda l, j: (l, 0, j)),
        out_shape=jax.ShapeDtypeStruct((depth, 8, n), F32),
        compiler_params=_cparams(("arbitrary", "arbitrary")),
        name="modulation",
    )(cvec, w_mod, b_mod.reshape(depth, 1, n))


def _rope(x, cos, sin_next, sin_prev, partner):
    return (x * cos + pltpu.roll(x, LANES - partner, 1) * sin_next
            + pltpu.roll(x, partner, 1) * sin_prev)


def _rms(x, w):
    return x * lax.rsqrt(jnp.mean(x * x, -1, keepdims=True) + EPS) * w


def _inproj_kernel(x_ref, mod_ref, win_ref, rope_ref, qn_ref, kvn_ref, wuq_ref, wuk_ref, wuv_ref,
                   h_ref, qd_ref, kd_ref, vd_ref, mlqk_ref, mlv_ref, mlo_ref, z_ref, xbc_ref,
                   small_ref, qm_ref, km_ref, vm_ref):
    x = x_ref[...]
    h = _ln(x) * (1.0 + mod_ref[1:2, :]) + mod_ref[0:1, :]
    hb = h.astype(BF16)
    h_ref[...] = hb

    def proj(lo, hi):
        return jnp.dot(hb, win_ref[:, lo:hi], preferred_element_type=F32)

    cos_d, sn_d, sp_d = rope_ref[:, 0:128], rope_ref[:, 128:256], rope_ref[:, 256:384]
    cos_m, sn_m, sp_m = rope_ref[:, 384:512], rope_ref[:, 512:640], rope_ref[:, 640:768]

    lane = lax.broadcasted_iota(jnp.int32, (x.shape[0], LANES), 1)
    one_col = (lane == 0).astype(BF16)

    qd = proj(C_QD, C_KD)
    kd = proj(C_KD, C_VD)
    vd = proj(C_VD, C_MLQK).astype(BF16)
    for hh in range(DIFF_H):
        sl = slice(hh * LANES, (hh + 1) * LANES)
        qd_ref[:, sl] = (_rope(qd[:, sl], cos_d, sn_d, sp_d, 16) * (DIFF_HD ** -0.5 * LOG2E)).astype(BF16)
        kd_ref[:, sl] = _rope(kd[:, sl], cos_d, sn_d, sp_d, 16).astype(BF16)
        vd_ref[:, 2 * hh * LANES:(2 * hh + 1) * LANES] = vd[:, sl]
        vd_ref[:, (2 * hh + 1) * LANES:(2 * hh + 2) * LANES] = one_col
    mlqk_ref[...] = proj(C_MLQK, C_MLV)
    mlv_ref[...] = proj(C_MLV, C_MLO).astype(BF16)
    mlo_ref[...] = proj(C_MLO, C_Z).astype(BF16)
    z_ref[...] = proj(C_Z, C_XBC).astype(BF16)
    xbc_ref[...] = proj(C_XBC, C_CQ)
    small = proj(C_SMALL, C_END)
    small_ref[...] = small

    cq = _rms(proj(C_CQ, C_CKV), qn_ref[...])
    ckv = _rms(proj(C_CKV, C_SMALL), kvn_ref[...])
    qm = _bdot(cq, wuq_ref[...])
    km = _bdot(ckv, wuk_ref[...])
    vm = _bdot(ckv, wuv_ref[...]).astype(BF16)
    kr = jnp.where((lane >= SM_KR) & (lane < SM_KR + MLA_ROPE), small, 0.0)
    kr = _rope(kr, cos_m, sn_m, sp_m, 8)
    scale = (MLA_NOPE + MLA_ROPE) ** -0.5 * LOG2E
    for hh in range(MLA_H):
        sl = slice(hh * LANES, (hh + 1) * LANES)
        qm_ref[:, sl] = (_rope(qm[:, sl], cos_m, sn_m, sp_m, 8) * scale).astype(BF16)
        km_ref[:, sl] = (km[:, sl] + kr).astype(BF16)
        vm_ref[:, 2 * hh * LANES:(2 * hh + 1) * LANES] = vm[:, sl]
        vm_ref[:, (2 * hh + 1) * LANES:(2 * hh + 2) * LANES] = one_col


def _inproj(xall, mod, nlt, w_in, rope, qn, kvn, wuq, wuk, wuv, tm):
    s, d = xall.shape
    n = s // tm
    row = lambda w: pl.BlockSpec((tm, w), lambda i: (i, 0))
    full = lambda a: pl.BlockSpec(a.shape, lambda i: (0,) * a.ndim)
    outs = [(d, BF16), (512, BF16), (512, BF16), (1024, BF16), (512, F32), (512, BF16), (512, BF16),
            (512, BF16), (768, F32), (128, F32), (512, BF16), (512, BF16), (1024, BF16)]
    return pl.pallas_call(
        _inproj_kernel,
        grid=(n,),
        in_specs=[row(d),
                  pl.BlockSpec((None, 8, d), lambda i: (jnp.where(i < nlt, 0, 1), 0, 0)),
                  full(w_in), row(768), full(qn), full(kvn), full(wuq), full(wuk), full(wuv)],
        out_specs=[row(w) for w, _ in outs],
        out_shape=[jax.ShapeDtypeStruct((s, w), dt) for w, dt in outs],
        compiler_params=_cparams(("arbitrary",)),
        name="inproj",
    )(xall, mod, w_in, rope, qn, kvn, wuq, wuk, wuv)


def _conv_kernel(cur_ref, prev_ref, next_ref, w_ref, b_ref, *rest, splits, scales, nlt, n):
    out_refs, ext_ref = rest[:-1], rest[-1]
    i = pl.program_id(0)
    tm = cur_ref.shape[0]
    has_prev = jnp.logical_and(i != 0, i != nlt).astype(F32)
    has_next = jnp.logical_and(i != nlt - 1, i != n - 1).astype(F32)
    ext_ref[0:CONV_HALO, :] = prev_ref[...] * has_prev
    ext_ref[CONV_HALO:CONV_HALO + tm, :] = cur_ref[...]
    ext_ref[CONV_HALO + tm:, :] = next_ref[...] * has_next
    acc = b_ref[...] + w_ref[0:1, :] * ext_ref[pl.ds(CONV_HALO - 2, tm), :]
    for k in range(1, CONV_K):
        acc = acc + w_ref[k:k + 1, :] * ext_ref[pl.ds(CONV_HALO - 2 + k, tm), :]
    y = acc * _sigmoid(acc)
    lo = 0
    for ref, wd, sc in zip(out_refs, splits, scales):
        ref[...] = (y[:, lo:lo + wd] * sc).astype(ref.dtype)
        lo += wd


def _conv_silu(xin, w, b, nlt, tm, splits, scales):
    s, c = xin.shape
    n = s // tm
    r = tm // CONV_HALO
    wpad = jnp.zeros((8, c), F32).at[:CONV_K].set(w)
    return pl.pallas_call(
        functools.partial(_conv_kernel, splits=splits, scales=scales, nlt=nlt, n=n),
        grid=(n,),
        in_specs=[pl.BlockSpec((tm, c), lambda i: (i, 0)),
                  pl.BlockSpec((CONV_HALO, c), lambda i: (jnp.maximum(i * r - 1, 0), 0)),
                  pl.BlockSpec((CONV_HALO, c), lambda i: (jnp.minimum((i + 1) * r, n * r - 1), 0)),
                  pl.BlockSpec((8, c), lambda i: (0, 0)),
                  pl.BlockSpec((1, c), lambda i: (0, 0))],
        out_specs=[pl.BlockSpec((tm, wd), lambda i: (i, 0)) for wd in splits],
        out_shape=[jax.ShapeDtypeStruct((s, wd), BF16) for wd in splits],
        scratch_shapes=[pltpu.VMEM((tm + 2 * CONV_HALO, c), F32)],
        compiler_params=_cparams(("arbitrary",)),
        name="conv_silu",
    )(xin, xin, xin, wpad, b.reshape(1, c))


def _flash_kernel(q_ref, k_ref, v_ref, lam_ref, sub_ref, o_ref, qs_ref, s_buf, p_buf, a_buf, m_ref, acc_ref,
                  *, diff, tk, nkv, lam_init):
    tq = q_ref.shape[0]
    if diff:
        q = q_ref[...]
        lane = lax.broadcasted_iota(jnp.int32, q.shape, 1)
        zero = jnp.zeros_like(q)
        qs_ref[0:tq, :] = jnp.where(lane < DIFF_HD, q, zero)
        qs_ref[tq:, :] = jnp.where(lane >= DIFF_HD, q, zero)
    else:
        qs_ref[...] = q_ref[...]
    m_ref[...] = jnp.full(m_ref.shape, -jnp.inf, F32)
    acc_ref[...] = jnp.zeros_like(acc_ref)

    def scores(j, slot):
        off = pl.multiple_of(j * tk, tk)
        s_buf[slot] = lax.dot_general(qs_ref[...], k_ref[pl.ds(off, tk), :], (((1,), (1,)), ((), ())),
                                      preferred_element_type=F32)

    def softmax(slot):
        s = s_buf[slot]
        m_old = m_ref[...]
        m_new = jnp.maximum(m_old, jnp.max(s, -1, keepdims=True))
        p_buf[slot] = jnp.exp2(s - m_new).astype(BF16)
        a_buf[slot] = jnp.exp2(m_old - m_new)
        m_ref[...] = m_new

    def values(j, slot):
        off = pl.multiple_of(j * tk, tk)
        acc_ref[...] = a_buf[slot] * acc_ref[...] + jnp.dot(p_buf[slot], v_ref[pl.ds(off, tk), :],
                                                            preferred_element_type=F32)

    scores(0, 0)
    if nkv > 1:
        scores(1, 1)
    softmax(0)

    def step(j, slot):
        scores(j + 1, 1 - slot)
        softmax(slot)
        values(j - 1, 1 - slot)

    def body(jj, carry):
        step(2 * jj + 1, 1)
        step(2 * jj + 2, 0)
        return carry

    pairs = max(nkv - 2, 0) // 2
    if pairs:
        lax.fori_loop(0, pairs, body, 0)
    if max(nkv - 2, 0) % 2:
        step(nkv - 2, (nkv - 2) % 2)
    if nkv > 1:
        softmax((nkv - 1) % 2)
        values(nkv - 2, nkv % 2)
    values(nkv - 1, (nkv - 1) % 2)

    acc = acc_ref[...]
    o = acc[:, :LANES] / acc[:, LANES:LANES + 1]
    if diff:
        lam = lam_ref[...]
        lam_full = (jnp.exp(jnp.sum(lam[0:1, :] * lam[1:2, :], -1, keepdims=True))
                    - jnp.exp(jnp.sum(lam[2:3, :] * lam[3:4, :], -1, keepdims=True)) + lam_init)
        o = o[:tq] - lam_full * o[tq:]
        o = _rms(o, sub_ref[...]) * (1.0 - lam_init)
    o_ref[...] = o.astype(o_ref.dtype)


def _flash(q, k, v, lam, subln, *, diff, q_tile0, nq, kv_block, kv_len, tq, tk, lam_init=0.0):
    heads = q.shape[1] // LANES
    rows = 2 * tq if diff else tq
    return pl.pallas_call(
        functools.partial(_flash_kernel, diff=diff, tk=tk, nkv=kv_len // tk, lam_init=lam_init),
        grid=(heads, nq),
        in_specs=[pl.BlockSpec((tq, LANES), lambda h, i: (i + q_tile0, h)),
                  pl.BlockSpec((kv_len, LANES), lambda h, i: (kv_block, h)),
                  pl.BlockSpec((kv_len, 2 * LANES), lambda h, i: (kv_block, h)),
                  pl.BlockSpec(lam.shape, lambda h, i: (0, 0)),
                  pl.BlockSpec(subln.shape, lambda h, i: (0, 0))],
        out_specs=pl.BlockSpec((tq, LANES), lambda h, i: (i, h)),
        out_shape=jax.ShapeDtypeStruct((nq * tq, heads * LANES), BF16),
        scratch_shapes=[pltpu.VMEM((rows, LANES), BF16),
                        pltpu.VMEM((2, rows, tk), F32),
                        pltpu.VMEM((2, rows, tk), BF16),
                        pltpu.VMEM((2, rows, 1), F32),
                        pltpu.VMEM((rows, 1), F32),
                        pltpu.VMEM((rows, 2 * LANES), F32)],
        compiler_params=_cparams(("arbitrary", "arbitrary")),
        name="flash_diff" if diff else "flash_mla",
    )(q, k, v, lam, subln)


def _attention(q, k, v, lam, subln, *, diff, ctx_len, ctx_out, tq, tk, lam_init=0.0):
    s = q.shape[0]
    t = s - ctx_len
    lat = _flash(q, k, v, lam, subln, diff=diff, q_tile0=0, nq=t // tq, kv_block=0, kv_len=s,
                 tq=tq, tk=tk, lam_init=lam_init)
    if not ctx_out:
        return lat, None
    tqc = math.gcd(tq, ctx_len)
    ctx = _flash(q, k, v, lam, subln, diff=diff, q_tile0=t // tqc, nq=ctx_len // tqc, kv_block=t // ctx_len,
                 kv_len=ctx_len, tq=tqc, tk=math.gcd(tk, ctx_len), lam_init=lam_init)
    return lat, ctx


def _tri(n, upper):
    r = lax.broadcasted_iota(jnp.int32, (n, n), 0)
    c = lax.broadcasted_iota(jnp.int32, (n, n), 1)
    return (c >= r) if upper else (c <= r)


def _mlstm_dir(bwd, q_ref, kt_ref, v_ref, sm_ref, smt_ref, gb_ref, gbt_ref, o_ref, c_ref, m_ref, first):
    cl = q_ref.shape[0]

    @pl.when(first)
    def _():
        c_ref[...] = jnp.zeros_like(c_ref)
        m_ref[...] = jnp.zeros_like(m_ref)

    allowed = _tri(cl, bwd)
    cum = allowed.astype(F32)
    pre = sm_ref[...] + gb_ref[...]
    pre_t = smt_ref[...] + gbt_ref[...]
    lf = -_softplus(-pre)
    lf_t = -_softplus(-pre_t)
    g_col = _hdot(cum, lf)
    g_row = _hdot(lf_t, cum.T)
    q = q_ref[...]
    kt = kt_ref[...]
    lane_q = lax.broadcasted_iota(jnp.int32, q.shape, 1)
    ones = jnp.ones((cl, LANES), BF16)
    end = 0 if bwd else cl - 1
    for hh in range(ML_H):
        ci, cf = (2 * bwd) * ML_H + hh, (2 * bwd + 1) * ML_H + hh
        g_c, g_r = g_col[:, cf:cf + 1], g_row[cf:cf + 1, :]
        i_c, i_r = pre[:, ci:ci + 1], pre_t[ci:ci + 1, :]
        m_prev = m_ref[hh:hh + 1, 0:1]
        qh = jnp.where((lane_q >= hh * ML_DK) & (lane_q < (hh + 1) * ML_DK), q, jnp.zeros_like(q))
        vh = jnp.concatenate([v_ref[:, hh * ML_DV:(hh + 1) * ML_DV], ones], 1)
        logw = jnp.where(allowed, g_c - g_r + i_r, -jnp.inf)
        log_inter = g_c + m_prev
        m_j = jnp.maximum(log_inter, jnp.max(logw, -1, keepdims=True))
        w = jnp.exp(logw - m_j) * jnp.dot(qh, kt, preferred_element_type=F32)
        a_inter = jnp.exp(log_inter - m_j)
        cstate = c_ref[...]
        nd = _bdot(w, vh) + a_inter * _bdot(qh, cstate)
        num, den = nd[:, :ML_DV], nd[:, ML_DV:ML_DV + 1]
        o_ref[:, hh * ML_DV:(hh + 1) * ML_DV] = num / jnp.maximum(jnp.abs(den), jnp.exp(-m_j))
        g_end = g_c[end:end + 1, :]
        log_s = g_end - g_c + i_c
        m_new = jnp.maximum(g_end + m_prev, jnp.max(log_s, 0, keepdims=True))
        ws = jnp.exp(log_s - m_new)
        decay = jnp.exp(g_end + m_prev - m_new)
        rows = slice(hh * ML_DK, (hh + 1) * ML_DK)
        upd = jnp.dot(kt[rows, :], (ws * vh.astype(F32)).astype(BF16), preferred_element_type=F32)
        c_ref[rows, :] = decay * cstate[rows, :] + upd
        m_ref[hh:hh + 1, :] = jnp.broadcast_to(m_new, (1, LANES))


def _mlstm_kernel(qf, ktf, vf, smf, smtf, qb, ktb, vb, smb, smtb, gb_ref, gbt_ref, of_ref, ob_ref,
                  cf_ref, mf_ref, cb_ref, mb_ref):
    first = pl.program_id(0) == 0
    _mlstm_dir(0, qf, ktf, vf, smf, smtf, gb_ref, gbt_ref, of_ref, cf_ref, mf_ref, first)
    _mlstm_dir(1, qb, ktb, vb, smb, smtb, gb_ref, gbt_ref, ob_ref, cb_ref, mb_ref, first)


def _scan_orders(nc, ncc):
    nlc = nc - ncc
    fwd = lambda c: jnp.where(c < ncc, nlc + c, c - ncc)
    bwd = lambda c: nc - 1 - c
    return fwd, bwd


def _mlstm_scan(q, kt, v, small, small_t, gate_b, ctx_len, cl):
    s = q.shape[0]
    nc, ncc = s // cl, ctx_len // cl
    gb = jnp.zeros((1, LANES), F32).at[0, SM_GATE:SM_GATE + 4 * ML_H].set(gate_b.reshape(-1))

    def specs(order):
        return [pl.BlockSpec((cl, ML_H * ML_DK), lambda c: (order(c), 0)),
                pl.BlockSpec((ML_H * ML_DK, cl), lambda c: (0, order(c))),
                pl.BlockSpec((cl, ML_H * ML_DV), lambda c: (order(c), 0)),
                pl.BlockSpec((cl, LANES), lambda c: (order(c), 0)),
                pl.BlockSpec((LANES, cl), lambda c: (0, order(c)))]

    fwd, bwd = _scan_orders(nc, ncc)
    const = lambda a: pl.BlockSpec(a.shape, lambda c: (0, 0))
    gbt = gb.reshape(LANES, 1)
    state = [pltpu.VMEM((ML_H * ML_DK, 2 * ML_DV), F32), pltpu.VMEM((8, LANES), F32)]
    return pl.pallas_call(
        _mlstm_kernel,
        grid=(nc,),
        in_specs=specs(fwd) + specs(bwd) + [const(gb), const(gbt)],
        out_specs=[pl.BlockSpec((cl, ML_H * ML_DV), lambda c: (fwd(c), 0)),
                   pl.BlockSpec((cl, ML_H * ML_DV), lambda c: (bwd(c), 0))],
        out_shape=[jax.ShapeDtypeStruct((s, ML_H * ML_DV), F32)] * 2,
        scratch_shapes=state + state,
        compiler_params=_cparams(("arbitrary",)),
        name="mlstm_scan",
    )(q, kt, v, small, small_t, q, kt, v, small, small_t, gb, gbt)


def _ssd_dir(bwd, xs_ref, bt_ref, cm_ref, sm_ref, smt_ref, db_ref, dbt_ref, al_ref, alt_ref, ex_ref,
             o_ref, st_ref, first):
    cl = xs_ref.shape[0]

    @pl.when(first)
    def _():
        st_ref[...] = jnp.zeros_like(st_ref)

    allowed = _tri(cl, bwd)
    cum = allowed.astype(F32)
    dt = _softplus(sm_ref[...] + db_ref[...])
    dt_t = _softplus(smt_ref[...] + dbt_ref[...])
    a = dt * (-jnp.exp(al_ref[...]))
    a_t = dt_t * (-jnp.exp(alt_ref[...]))
    s_col = _hdot(cum, a)
    s_row = _hdot(a_t, cum.T)
    end = 0 if bwd else cl - 1
    s_end = s_col[end:end + 1, :]
    stack = jnp.concatenate([jnp.exp(s_col), dt, dt * jnp.exp(s_end - s_col), jnp.exp(jnp.broadcast_to(s_end, (8, LANES)))], 0)
    wide = _hdot(stack, ex_ref[bwd])
    es_w, dt_w, wt_w, dec_w = wide[:cl], wide[cl:2 * cl], wide[2 * cl:3 * cl], wide[3 * cl:3 * cl + 1]
    xs = xs_ref[...].astype(F32)
    xdt = (xs * dt_w).astype(BF16)
    cm = cm_ref[...]
    bt = bt_ref[...]
    lane_c = lax.broadcasted_iota(jnp.int32, cm.shape, 1)
    lane_x = lax.broadcasted_iota(jnp.int32, (cl, LANES), 1)
    state = st_ref[...]
    y = es_w * _bdot(cm, state)
    per_pair = SSD_H // (D_INNER // LANES)
    for blk in range(D_INNER // LANES):
        grp = (blk * per_pair) // (SSD_H // SSD_G)
        cg = jnp.where((lane_c >= grp * SSD_N) & (lane_c < (grp + 1) * SSD_N), cm, jnp.zeros_like(cm))
        cb = jnp.dot(cg, bt, preferred_element_type=F32)
        xblk = xdt[:, blk * LANES:(blk + 1) * LANES]
        res = []
        for sub in range(per_pair):
            hh = blk * per_pair + sub
            col = SM_DT + bwd * SSD_H + hh
            decay = jnp.exp(jnp.where(allowed, s_col[:, col:col + 1] - s_row[col:col + 1, :], -jnp.inf))
            res.append(_bdot(decay * cb, xblk))
        intra = jnp.where(lane_x < SSD_P, res[0], res[1])
        o_ref[:, blk * LANES:(blk + 1) * LANES] = y[:, blk * LANES:(blk + 1) * LANES] + intra
    upd = jnp.dot(bt, (xs * wt_w).astype(BF16), preferred_element_type=F32)
    r = lax.broadcasted_iota(jnp.int32, upd.shape, 0) // SSD_N
    c = lax.broadcasted_iota(jnp.int32, upd.shape, 1) // (SSD_P * (SSD_H // SSD_G))
    st_ref[...] = jnp.where(r == c, dec_w * state + upd, 0.0)


def _ssd_kernel(xf, btf, cf, smf, smtf, xb, btb, cb, smb, smtb, db_ref, dbt_ref, al_ref, alt_ref, ex_ref,
                of_ref, ob_ref, sf_ref, sb_ref):
    first = pl.program_id(0) == 0
    _ssd_dir(0, xf, btf, cf, smf, smtf, db_ref, dbt_ref, al_ref, alt_ref, ex_ref, of_ref, sf_ref, first)
    _ssd_dir(1, xb, btb, cb, smb, smtb, db_ref, dbt_ref, al_ref, alt_ref, ex_ref, ob_ref, sb_ref, first)


def _ssd_scan(xs, bt, cm, small, small_t, dt_bias, a_log, ctx_len, cl):
    s = xs.shape[0]
    nc, ncc = s // cl, ctx_len // cl
    db = jnp.zeros((1, LANES), F32).at[0, SM_DT:SM_DT + 2 * SSD_H].set(dt_bias.reshape(-1))
    al = jnp.zeros((1, LANES), F32).at[0, SM_DT:SM_DT + 2 * SSD_H].set(a_log.reshape(-1))
    src = jnp.arange(LANES)[None, :, None]
    dst_head = (jnp.arange(D_INNER) // SSD_P)[None, None, :]
    dirs = jnp.arange(2)[:, None, None]
    expand = (src == SM_DT + dirs * SSD_H + dst_head).astype(F32)

    def specs(order):
        return [pl.BlockSpec((cl, D_INNER), lambda c: (order(c), 0)),
                pl.BlockSpec((SSD_G * SSD_N, cl), lambda c: (0, order(c))),
                pl.BlockSpec((cl, SSD_G * SSD_N), lambda c: (order(c), 0)),
                pl.BlockSpec((cl, LANES), lambda c: (order(c), 0)),
                pl.BlockSpec((LANES, cl), lambda c: (0, order(c)))]

    fwd, bwd = _scan_orders(nc, ncc)
    const = lambda a: pl.BlockSpec(a.shape, lambda c: (0,) * a.ndim)
    dbt, alt = db.reshape(LANES, 1), al.reshape(LANES, 1)
    state = pltpu.VMEM((SSD_G * SSD_N, D_INNER), F32)
    return pl.pallas_call(
        _ssd_kernel,
        grid=(nc,),
        in_specs=specs(fwd) + specs(bwd) + [const(db), const(dbt), const(al), const(alt), const(expand)],
        out_specs=[pl.BlockSpec((cl, D_INNER), lambda c: (fwd(c), 0)),
                   pl.BlockSpec((cl, D_INNER), lambda c: (bwd(c), 0))],
        out_shape=[jax.ShapeDtypeStruct((s, D_INNER), F32)] * 2,
        scratch_shapes=[state, state],
        compiler_params=_cparams(("arbitrary",)),
        name="ssd_scan",
    )(xs, bt, cm, small, small_t, xs, bt, cm, small, small_t, db, dbt, al, alt, expand)


def _merge_kernel(x_ref, mod_ref, h_ref, a_ref, hf_ref, hb_ref, mlo_ref, mln_ref, m_ref, yf_ref, yb_ref,
                  xs_ref, z_ref, dsk_ref, sdn_ref, wg_ref, bg_ref, wb_ref, wo_ref, g_ref, b_ref, o_ref,
                  *, alpha):
    hb16 = h_ref[...]
    hm = hf_ref[...] + hb_ref[...]
    og = _sigmoid(mlo_ref[...].astype(F32))
    b_parts = []
    for hh in range(ML_H):
        sl = slice(hh * ML_DV, (hh + 1) * ML_DV)
        b_parts.append(_rms(hm[:, sl], mln_ref[:, sl]) * og[:, sl])
    b_br = jnp.concatenate(b_parts, 1)
    z = z_ref[...].astype(F32)
    ys = (yf_ref[...] + yb_ref[...] + dsk_ref[...] * xs_ref[...].astype(F32)) * (z * _sigmoid(z))
    gw = D_INNER // SSD_G
    s_br = jnp.concatenate([_rms(ys[:, g * gw:(g + 1) * gw], sdn_ref[:, g * gw:(g + 1) * gw])
                            for g in range(SSD_G)], 1)
    branches = (a_ref[...], b_br, m_ref[...], s_br)
    y = None
    for k, br in enumerate(branches):
        gate = _sigmoid(jnp.dot(hb16, wg_ref[k], preferred_element_type=F32) + bg_ref[k:k + 1, :])
        term = gate * _bdot(br, wb_ref[k])
        y = term if y is None else y + term
    y = _bdot(y, wo_ref[...])
    x1 = alpha * x_ref[...] + mod_ref[2:3, :] * y
    o_ref[...] = _ln(x1) * g_ref[...] + b_ref[...]


def _merge(xall, mod, nlt, n_tiles, h, a, hf, hb, mlo, mln, m, yf, yb, xs, z, dsk, sdn, wg, bg, wb, wo,
           g, b, tm, alpha):
    s, d = xall.shape
    row = lambda w: pl.BlockSpec((tm, w), lambda i: (i, 0))
    full = lambda arr: pl.BlockSpec(arr.shape, lambda i: (0,) * arr.ndim)
    return pl.pallas_call(
        functools.partial(_merge_kernel, alpha=alpha),
        grid=(n_tiles,),
        in_specs=[row(d), pl.BlockSpec((None, 8, d), lambda i: (jnp.where(i < nlt, 0, 1), 0, 0)),
                  row(d), row(512), row(512), row(512), row(512), full(mln), row(512), row(512), row(512),
                  row(512), row(512), full(dsk), full(sdn), full(wg), full(bg), full(wb), full(wo),
                  full(g), full(b)],
        out_specs=row(d),
        out_shape=jax.ShapeDtypeStruct((n_tiles * tm, d), F32),
        compiler_params=_cparams(("arbitrary",)),
        name="merge_ln1",
    )(xall, mod, h, a, hf, hb, mlo, mln, m, yf, yb, xs, z, dsk, sdn, wg, bg, wb, wo, g, b)


def _mlp_kernel(x_ref, mod_ref, wu_ref, bu_ref, wd_ref, bd_ref, g_ref, b_ref, o_ref, *, alpha):
    x = x_ref[...]
    hm = _ln(x) * (1.0 + mod_ref[4:5, :]) + mod_ref[3:4, :]
    u = jnp.maximum(_bdot(hm, wu_ref[...]) + bu_ref[...], 0.0)
    f = _bdot(u * u, wd_ref[...]) + bd_ref[...]
    x2 = alpha * x + mod_ref[5:6, :] * f
    o_ref[...] = _ln(x2) * g_ref[...] + b_ref[...]


def _mlp(x1, mod, nlt, wu, bu, wd, bd, g, b, tm, alpha):
    s, d = x1.shape
    n_tiles = s // tm
    full = lambda arr: pl.BlockSpec(arr.shape, lambda i: (0,) * arr.ndim)
    return pl.pallas_call(
        functools.partial(_mlp_kernel, alpha=alpha),
        grid=(n_tiles,),
        in_specs=[pl.BlockSpec((tm, d), lambda i: (i, 0)),
                  pl.BlockSpec((None, 8, d), lambda i: (jnp.where(i < nlt, 0, 1), 0, 0)),
                  full(wu), full(bu), full(wd), full(bd), full(g), full(b)],
        out_specs=pl.BlockSpec((tm, d), lambda i: (i, 0)),
        out_shape=jax.ShapeDtypeStruct((n_tiles * tm, d), F32),
        compiler_params=_cparams(("arbitrary",)),
        name="mlp_ln2",
    )(x1, mod, wu, bu, wd, bd, g, b)


def _axis_tables(pos, dsub):
    half = dsub // 2
    inv = ROPE_BASE ** (-jnp.arange(half, dtype=F32) * 2.0 / dsub)
    ang = pos[:, None] * inv[None, :]
    c, s = jnp.cos(ang), jnp.sin(ang)
    z = jnp.zeros_like(s)
    return jnp.concatenate([c, c], -1), jnp.concatenate([-s, z], -1), jnp.concatenate([z, s], -1)


def _rope_tables(t, ctx_len):
    pos = jnp.arange(t, dtype=jnp.int32)
    row, col = (pos // GRID_W).astype(F32), (pos % GRID_W).astype(F32)

    def two_axis(dsub):
        r, c = _axis_tables(row, dsub), _axis_tables(col, dsub)
        return [jnp.concatenate([a, b], -1) for a, b in zip(r, c)]

    cd, nd, pd = [jnp.tile(a, (1, 2)) for a in two_axis(DIFF_HD // 2)]
    cm, nm, pm = two_axis(MLA_ROPE // 2)
    ones = lambda w: jnp.ones((t, w), F32)
    zeros = lambda w: jnp.zeros((t, w), F32)
    pad = LANES - MLA_NOPE - MLA_ROPE
    cm = jnp.concatenate([ones(MLA_NOPE), cm, ones(pad)], -1)
    nm = jnp.concatenate([zeros(MLA_NOPE), nm, zeros(pad)], -1)
    pm = jnp.concatenate([zeros(MLA_NOPE), pm, zeros(pad)], -1)
    lat = jnp.concatenate([cd, nd, pd, cm, nm, pm], -1)
    ident = jnp.concatenate([jnp.ones((ctx_len, LANES), F32), jnp.zeros((ctx_len, 2 * LANES), F32)] * 2, -1)
    return jnp.concatenate([lat, ident], 0)


def _relayout_w_in(w):
    d = w.shape[0]
    o = [0, 512, 1024, 1536, 2048, 2560, 3072, 3088, 3472, 3728, 3760, 4272, 5040, 5056]
    seg = lambda i: w[:, o[i]:o[i + 1]]
    qd, kd, vd, mlqk, mlv, mlo, mlg, cq, ckv, kr, z, xbc, dt = [seg(i) for i in range(13)]
    zc = lambda n: jnp.zeros((d, n), w.dtype)
    small = jnp.concatenate([mlg, dt, zc(SM_KR - 32), kr, zc(LANES - SM_KR - MLA_ROPE)], 1)
    return jnp.concatenate([qd, kd, vd, mlqk, mlv, mlo, z, xbc, cq, ckv, small], 1).astype(BF16)


def _pad_heads(w, heads, width):
    r = w.shape[0]
    w = w.reshape(r, heads, width)
    return jnp.pad(w, ((0, 0), (0, 0), (0, LANES - width))).reshape(r, heads * LANES)


def _pick_tile(n, pref):
    for t in pref:
        if n % t == 0:
            return t
    raise ValueError(f"no tile for {n}")


def kernel(x, c, ctx, c_ctx, w_mod, b_mod, w_in, diff_lambda, diff_subln, ml_conv_w, ml_conv_b, ml_gate_b, ml_norm, mla_q_norm, mla_kv_norm, mla_w_uq, mla_w_ukv, ssd_conv_w, ssd_conv_b, ssd_dt_bias, ssd_a_log, ssd_d, ssd_norm, w_gate, b_gate, w_branch, w_o, ln1_g, ln1_b, w_up, b_up, w_down, b_down, ln2_g, ln2_b):
    assert x.shape[0] == 1 and ctx.shape[0] == 1
    depth = w_in.shape[0]
    t, d = x.shape[1], x.shape[2]
    ctx_len = ctx.shape[1]
    s = ctx_len + t
    tm = _pick_tile(math.gcd(ctx_len, t), (256, 128))
    cl = 128
    tq_diff = _pick_tile(t, (256, 128))
    tq_mla = _pick_tile(t, (512, 256, 128))
    tk = _pick_tile(s, (1280, 640, 256, 128))
    assert ctx_len % cl == 0 and t % cl == 0 and t % GRID_W == 0 and t % ctx_len == 0
    nlt, n = t // tm, s // tm
    alpha = (2 * depth) ** 0.25

    xall = jnp.concatenate([x[0], ctx[0]], 0)
    cvec = jnp.zeros((8, d), F32).at[0].set(c[0]).at[1].set(c_ctx)
    mod = _modulation(cvec, w_mod, b_mod)
    mod = jnp.pad(mod[:, :2].reshape(depth, 2, 6, d), ((0, 0), (0, 0), (0, 2), (0, 0)))
    rope = _rope_tables(t, ctx_len)

    for l in range(depth):
        ctx_out = l < depth - 1
        lam_init = 0.8 - 0.6 * math.exp(-0.3 * l)
        wq = _pad_heads(mla_w_uq[l], MLA_H, MLA_NOPE + MLA_ROPE).astype(BF16)
        wkv = mla_w_ukv[l].reshape(MLA_KV_RANK, MLA_H, MLA_NOPE + MLA_V)
        wk = _pad_heads(wkv[:, :, :MLA_NOPE].reshape(MLA_KV_RANK, -1), MLA_H, MLA_NOPE).astype(BF16)
        wv = wkv[:, :, MLA_NOPE:].reshape(MLA_KV_RANK, -1).astype(BF16)
        (h, qd, kd, vd, mlqk, mlv, mlo, z, xbc, small, qm, km, vm) = _inproj(
            xall, mod[l], nlt, _relayout_w_in(w_in[l]), rope, mla_q_norm[l][None], mla_kv_norm[l][None],
            wq, wk, wv, tm)

        a_lat, a_ctx = _attention(qd, kd, vd, diff_lambda[l], diff_subln[l][None], diff=True,
                                  ctx_len=ctx_len, ctx_out=ctx_out, tq=tq_diff, tk=tk, lam_init=lam_init)
        m_lat, m_ctx = _attention(qm, km, vm, diff_lambda[l], diff_subln[l][None], diff=False,
                                  ctx_len=ctx_len, ctx_out=ctx_out, tq=tq_mla, tk=tk)

        ml_q, ml_k = _conv_silu(mlqk, ml_conv_w[l], ml_conv_b[l], nlt, tm,
                                (ML_H * ML_DK, ML_H * ML_DK), (1.0, ML_DK ** -0.5))
        xs, bm, cm = _conv_silu(xbc, ssd_conv_w[l], ssd_conv_b[l], nlt, tm,
                                (D_INNER, SSD_G * SSD_N, SSD_G * SSD_N), (1.0, 1.0, 1.0))
        small_t = small.T
        hf, hb = _mlstm_scan(ml_q, ml_k.T, mlv, small, small_t, ml_gate_b[l], ctx_len, cl)
        yf, yb = _ssd_scan(xs, bm.T, cm, small, small_t, ssd_dt_bias[l], ssd_a_log[l], ctx_len, cl)

        if ctx_out:
            a_all = jnp.concatenate([a_lat, a_ctx], 0)
            m_all = jnp.concatenate([m_lat, m_ctx], 0)
            n_tiles = n
        else:
            a_all, m_all = a_lat, m_lat
            n_tiles = nlt
        x1 = _merge(xall, mod[l], nlt, n_tiles, h, a_all, hf, hb, mlo, ml_norm[l][None], m_all, yf, yb,
                    xs, z, jnp.repeat(ssd_d[l], SSD_P)[None], ssd_norm[l][None], w_gate[l].astype(BF16),
                    b_gate[l], w_branch[l].astype(BF16), w_o[l].astype(BF16), ln1_g[l][None], ln1_b[l][None],
                    tm, alpha)
        xall = _mlp(x1, mod[l], nlt, w_up[l].astype(BF16), b_up[l][None],
                    w_down[l].astype(BF16), b_down[l][None], ln2_g[l][None], ln2_b[l][None], tm, alpha)
    return xall[None]
```

```python
import functools
import math

import jax
import jax.numpy as jnp
from jax import lax
from jax.experimental import pallas as pl
from jax.experimental.pallas import tpu as pltpu

F32 = jnp.float32
BF16 = jnp.bfloat16

GRID_W = 64
ROPE_BASE = 10000.0
CONV_K = 5
EPS = 1e-6
LOG2E = math.log2(math.e)
LANES = 128
CONV_HALO = 8

DIFF_H, DIFF_HD = 4, 64
ML_H, ML_DK, ML_DV = 4, 64, 128
MLA_H, MLA_Q_RANK, MLA_KV_RANK, MLA_NOPE, MLA_ROPE, MLA_V = 4, 384, 256, 64, 32, 128
SSD_H, SSD_P, SSD_G, SSD_N = 8, 64, 2, 64
D_INNER = SSD_H * SSD_P

C_QD, C_KD, C_VD, C_MLQK, C_MLV, C_MLO, C_Z, C_XBC, C_CQ, C_CKV, C_SMALL, C_END = (
    0, 512, 1024, 1536, 2048, 2560, 3072, 3584, 4352, 4736, 4992, 5120)
SM_GATE, SM_DT, SM_KR = 0, 16, 64

VMEM_LIMIT = 60 * 1024 * 1024


def _cparams(sem):
    return pltpu.CompilerParams(dimension_semantics=sem, vmem_limit_bytes=VMEM_LIMIT)


def _ln(x):
    mu = jnp.mean(x, -1, keepdims=True)
    xc = x - mu
    var = jnp.mean(xc * xc, -1, keepdims=True)
    return xc * lax.rsqrt(var + EPS)


def _sigmoid(x):
    return 1.0 / (1.0 + jnp.exp(-x))


def _softplus(x):
    return jnp.maximum(x, 0.0) + jnp.log1p(jnp.exp(-jnp.abs(x)))


def _bdot(a, b):
    return jnp.dot(a.astype(BF16), b.astype(BF16), preferred_element_type=F32)


def _hdot(a, b):
    return jnp.dot(a, b, precision=lax.Precision.HIGHEST, preferred_element_type=F32)


def _select_dot(x, sel, x_is_lhs):
    sel = sel.astype(BF16)
    acc, rest = None, x
    for _ in range(3):
        part = rest.astype(BF16)
        prod = (jnp.dot(part, sel, preferred_element_type=F32) if x_is_lhs
                else jnp.dot(sel, part, preferred_element_type=F32))
        acc = prod if acc is None else acc + prod
        rest = rest - part.astype(F32)
    return acc


def _mod_kernel(c_ref, w_ref, b_ref, o_ref):
    c = c_ref[...]
    o_ref[...] = _hdot(c * _sigmoid(c), w_ref[...]) + b_ref[...]


def _modulation(cvec, w_mod, b_mod):
    depth, d, n = w_mod.shape
    tn = 1536
    return pl.pallas_call(
        _mod_kernel,
        grid=(depth, n // tn),
        in_specs=[pl.BlockSpec((8, d), lambda l, j: (0, 0)),
                  pl.BlockSpec((None, d, tn), lambda l, j: (l, 0, j)),
                  pl.BlockSpec((None, 1, tn), lambda l, j: (l, 0, j))],
        out_specs=pl.BlockSpec((None, 8, tn), lambda l, j: (l, 0, j)),
        out_shape=jax.ShapeDtypeStruct((depth, 8, n), F32),
        compiler_params=_cparams(("arbitrary", "arbitrary")),
        name="modulation",
    )(cvec, w_mod, b_mod.reshape(depth, 1, n))


def _rope(x, cos, sin_next, sin_prev, partner):
    return (x * cos + pltpu.roll(x, LANES - partner, 1) * sin_next
            + pltpu.roll(x, partner, 1) * sin_prev)


def _rms(x, w):
    return x * lax.rsqrt(jnp.mean(x * x, -1, keepdims=True) + EPS) * w


def _inproj_kernel(xl_ref, xc_ref, mod_ref, win_ref, rope_ref, qn_ref, kvn_ref, wuq_ref, wuk_ref, wuv_ref,
                   h_ref, qd_ref, kd_ref, vd_ref, mlqk_ref, mlv_ref, mlo_ref, z_ref, xbc_ref,
                   small_ref, qm_ref, km_ref, vm_ref, *, nlt):
    x = _stream_tile(xl_ref, xc_ref, nlt)
    h = _ln(x) * (1.0 + mod_ref[1:2, :]) + mod_ref[0:1, :]
    hb = h.astype(BF16)
    h_ref[...] = hb

    def proj(lo, hi):
        return jnp.dot(hb, win_ref[:, lo:hi], preferred_element_type=F32)

    cos_d, sn_d, sp_d = rope_ref[:, 0:128], rope_ref[:, 128:256], rope_ref[:, 256:384]
    cos_m, sn_m, sp_m = rope_ref[:, 384:512], rope_ref[:, 512:640], rope_ref[:, 640:768]

    lane = lax.broadcasted_iota(jnp.int32, (x.shape[0], LANES), 1)
    one_col = (lane == 0).astype(BF16)

    qd = proj(C_QD, C_KD)
    kd = proj(C_KD, C_VD)
    vd = proj(C_VD, C_MLQK).astype(BF16)
    for hh in range(DIFF_H):
        sl = slice(hh * LANES, (hh + 1) * LANES)
        qd_ref[:, sl] = (_rope(qd[:, sl], cos_d, sn_d, sp_d, 16) * (DIFF_HD ** -0.5 * LOG2E)).astype(BF16)
        kd_ref[:, sl] = _rope(kd[:, sl], cos_d, sn_d, sp_d, 16).astype(BF16)
        vd_ref[:, 2 * hh * LANES:(2 * hh + 1) * LANES] = vd[:, sl]
        vd_ref[:, (2 * hh + 1) * LANES:(2 * hh + 2) * LANES] = one_col
    mlqk_ref[...] = proj(C_MLQK, C_MLV)
    mlv_ref[...] = proj(C_MLV, C_MLO).astype(BF16)
    mlo_ref[...] = proj(C_MLO, C_Z).astype(BF16)
    z_ref[...] = proj(C_Z, C_XBC).astype(BF16)
    xbc_ref[...] = proj(C_XBC, C_CQ)
    small = proj(C_SMALL, C_END)
    small_ref[...] = small

    cq = _rms(proj(C_CQ, C_CKV), qn_ref[...])
    ckv = _rms(proj(C_CKV, C_SMALL), kvn_ref[...])
    qm = _bdot(cq, wuq_ref[...])
    km = _bdot(ckv, wuk_ref[...])
    vm = _bdot(ckv, wuv_ref[...]).astype(BF16)
    kr = jnp.where((lane >= SM_KR) & (lane < SM_KR + MLA_ROPE), small, 0.0)
    kr = _rope(kr, cos_m, sn_m, sp_m, 8)
    scale = (MLA_NOPE + MLA_ROPE) ** -0.5 * LOG2E
    for hh in range(MLA_H):
        sl = slice(hh * LANES, (hh + 1) * LANES)
        qm_ref[:, sl] = (_rope(qm[:, sl], cos_m, sn_m, sp_m, 8) * scale).astype(BF16)
        km_ref[:, sl] = (km[:, sl] + kr).astype(BF16)
        vm_ref[:, 2 * hh * LANES:(2 * hh + 1) * LANES] = vm[:, sl]
        vm_ref[:, (2 * hh + 1) * LANES:(2 * hh + 2) * LANES] = one_col


def _stream_specs(tm, d, nlt, ctx_tile0):
    return [pl.BlockSpec((tm, d), lambda i: (jnp.minimum(i, nlt - 1), 0)),
            pl.BlockSpec((tm, d), lambda i: (jnp.maximum(i - nlt, 0) + ctx_tile0, 0))]


def _stream_tile(xl_ref, xc_ref, nlt):
    return jnp.where(pl.program_id(0) < nlt, xl_ref[...], xc_ref[...])


def _inproj(x_lat, x_ctx, ctx_tile0, s, mod, nlt, w_in, rope, qn, kvn, wuq, wuk, wuv, tm):
    d = x_lat.shape[1]
    n = s // tm
    row = lambda w: pl.BlockSpec((tm, w), lambda i: (i, 0))
    full = lambda a: pl.BlockSpec(a.shape, lambda i: (0,) * a.ndim)
    outs = [(d, BF16), (512, BF16), (512, BF16), (1024, BF16), (512, F32), (512, BF16), (512, BF16),
            (512, BF16), (768, F32), (128, F32), (512, BF16), (512, BF16), (1024, BF16)]
    return pl.pallas_call(
        functools.partial(_inproj_kernel, nlt=nlt),
        grid=(n,),
        in_specs=_stream_specs(tm, d, nlt, ctx_tile0) + [
            pl.BlockSpec((None, 8, d), lambda i: (jnp.where(i < nlt, 0, 1), 0, 0)),
            full(w_in), row(768), full(qn), full(kvn), full(wuq), full(wuk), full(wuv)],
        out_specs=[row(w) for w, _ in outs],
        out_shape=[jax.ShapeDtypeStruct((s, w), dt) for w, dt in outs],
        compiler_params=_cparams(("arbitrary",)),
        name="inproj",
    )(x_lat, x_ctx, mod, w_in, rope, qn, kvn, wuq, wuk, wuv)


def _conv_kernel(cur_ref, prev_ref, next_ref, w_ref, b_ref, *rest, splits, scales, nlt, n):
    out_refs, ext_ref = rest[:-1], rest[-1]
    i = pl.program_id(0)
    tm = cur_ref.shape[0]
    has_prev = jnp.logical_and(i != 0, i != nlt).astype(F32)
    has_next = jnp.logical_and(i != nlt - 1, i != n - 1).astype(F32)
    ext_ref[0:CONV_HALO, :] = prev_ref[...] * has_prev
    ext_ref[CONV_HALO:CONV_HALO + tm, :] = cur_ref[...]
    ext_ref[CONV_HALO + tm:, :] = next_ref[...] * has_next
    acc = b_ref[...] + w_ref[0:1, :] * ext_ref[pl.ds(CONV_HALO - 2, tm), :]
    for k in range(1, CONV_K):
        acc = acc + w_ref[k:k + 1, :] * ext_ref[pl.ds(CONV_HALO - 2 + k, tm), :]
    y = acc * _sigmoid(acc)
    lo = 0
    for ref, wd, sc in zip(out_refs, splits, scales):
        ref[...] = (y[:, lo:lo + wd] * sc).astype(ref.dtype)
        lo += wd


def _conv_silu(xin, w, b, nlt, tm, splits, scales):
    s, c = xin.shape
    n = s // tm
    r = tm // CONV_HALO
    wpad = jnp.zeros((8, c), F32).at[:CONV_K].set(w)
    return pl.pallas_call(
        functools.partial(_conv_kernel, splits=splits, scales=scales, nlt=nlt, n=n),
        grid=(n,),
        in_specs=[pl.BlockSpec((tm, c), lambda i: (i, 0)),
                  pl.BlockSpec((CONV_HALO, c), lambda i: (jnp.maximum(i * r - 1, 0), 0)),
                  pl.BlockSpec((CONV_HALO, c), lambda i: (jnp.minimum((i + 1) * r, n * r - 1), 0)),
                  pl.BlockSpec((8, c), lambda i: (0, 0)),
                  pl.BlockSpec((1, c), lambda i: (0, 0))],
        out_specs=[pl.BlockSpec((tm, wd), lambda i: (i, 0)) for wd in splits],
        out_shape=[jax.ShapeDtypeStruct((s, wd), BF16) for wd in splits],
        scratch_shapes=[pltpu.VMEM((tm + 2 * CONV_HALO, c), F32)],
        compiler_params=_cparams(("arbitrary",)),
        name="conv_silu",
    )(xin, xin, xin, wpad, b.reshape(1, c))


def _flash_kernel(q_ref, k_ref, v_ref, lam_ref, sub_ref, o_ref, qs_ref, s_buf, p_buf, a_buf, mx_buf, m_ref, acc_ref,
                  *, diff, tk, nkv, lam_init):
    tq = q_ref.shape[0]
    if diff:
        q = q_ref[...]
        lane = lax.broadcasted_iota(jnp.int32, q.shape, 1)
        zero = jnp.zeros_like(q)
        qs_ref[0:tq, :] = jnp.where(lane < DIFF_HD, q, zero)
        qs_ref[tq:, :] = jnp.where(lane >= DIFF_HD, q, zero)
    else:
        qs_ref[...] = q_ref[...]
    m_ref[...] = jnp.full(m_ref.shape, -jnp.inf, F32)
    acc_ref[...] = jnp.zeros_like(acc_ref)

    def scores(j, slot):
        off = pl.multiple_of(j * tk, tk)
        s_buf[slot] = lax.dot_general(qs_ref[...], k_ref[pl.ds(off, tk), :], (((1,), (1,)), ((), ())),
                                      preferred_element_type=F32)

    def softmax(slot):
        s = s_buf[slot]
        m_old = m_ref[...]
        m_new = jnp.maximum(m_old, jnp.max(s, -1, keepdims=True))
        p_buf[slot] = jnp.exp2(s - m_new).astype(BF16)
        a_buf[slot] = jnp.exp2(m_old - m_new)
        m_ref[...] = m_new

    def values(j, slot):
        off = pl.multiple_of(j * tk, tk)
        acc_ref[...] = a_buf[slot] * acc_ref[...] + jnp.dot(p_buf[slot], v_ref[pl.ds(off, tk), :],
                                                            preferred_element_type=F32)

    scores(0, 0)
    if nkv > 1:
        scores(1, 1)
    softmax(0)

    def step(j, slot):
        scores(j + 1, 1 - slot)
        softmax(slot)
        values(j - 1, 1 - slot)

    def body(jj, carry):
        step(2 * jj + 1, 1)
        step(2 * jj + 2, 0)
        return carry

    pairs = max(nkv - 2, 0) // 2
    if pairs:
        lax.fori_loop(0, pairs, body, 0, unroll=True)
    if max(nkv - 2, 0) % 2:
        step(nkv - 2, (nkv - 2) % 2)
    if nkv > 1:
        softmax((nkv - 1) % 2)
        values(nkv - 2, nkv % 2)
    values(nkv - 1, (nkv - 1) % 2)

    acc = acc_ref[...]
    o = acc[:, :LANES] / acc[:, LANES:LANES + 1]
    if diff:
        lam = lam_ref[...]
        lam_full = (jnp.exp(jnp.sum(lam[0:1, :] * lam[1:2, :], -1, keepdims=True))
                    - jnp.exp(jnp.sum(lam[2:3, :] * lam[3:4, :], -1, keepdims=True)) + lam_init)
        o = o[:tq] - lam_full * o[tq:]
        o = _rms(o, sub_ref[...]) * (1.0 - lam_init)
    o_ref[...] = o.astype(o_ref.dtype)


def _flash(q, k, v, lam, subln, *, diff, q_tile0, nq, kv_block, kv_len, tq, tk, lam_init=0.0):
    heads = q.shape[1] // LANES
    rows = 2 * tq if diff else tq
    return pl.pallas_call(
        functools.partial(_flash_kernel, diff=diff, tk=tk, nkv=kv_len // tk, lam_init=lam_init),
        grid=(heads, nq),
        in_specs=[pl.BlockSpec((tq, LANES), lambda h, i: (i + q_tile0, h)),
                  pl.BlockSpec((kv_len, LANES), lambda h, i: (kv_block, h)),
                  pl.BlockSpec((kv_len, 2 * LANES), lambda h, i: (kv_block, h)),
                  pl.BlockSpec(lam.shape, lambda h, i: (0, 0)),
                  pl.BlockSpec(subln.shape, lambda h, i: (0, 0))],
        out_specs=pl.BlockSpec((tq, LANES), lambda h, i: (i, h)),
        out_shape=jax.ShapeDtypeStruct((nq * tq, heads * LANES), BF16),
        scratch_shapes=[pltpu.VMEM((rows, LANES), BF16),
                        pltpu.VMEM((2, rows, tk), F32),
                        pltpu.VMEM((2, rows, tk), BF16),
                        pltpu.VMEM((2, rows, 1), F32),
                        pltpu.VMEM((2, rows, 1), F32),
                        pltpu.VMEM((rows, 1), F32),
                        pltpu.VMEM((rows, 2 * LANES), F32)],
        compiler_params=_cparams(("arbitrary", "arbitrary")),
        name="flash_diff" if diff else "flash_mla",
    )(q, k, v, lam, subln)


def _attention(q, k, v, lam, subln, *, diff, ctx_len, ctx_out, tq, tk, lam_init=0.0):
    s = q.shape[0]
    t = s - ctx_len
    lat = _flash(q, k, v, lam, subln, diff=diff, q_tile0=0, nq=t // tq, kv_block=0, kv_len=s,
                 tq=tq, tk=tk, lam_init=lam_init)
    if not ctx_out:
        return lat, None
    tqc = math.gcd(tq, ctx_len)
    ctx = _flash(q, k, v, lam, subln, diff=diff, q_tile0=t // tqc, nq=ctx_len // tqc, kv_block=t // ctx_len,
                 kv_len=ctx_len, tq=tqc, tk=math.gcd(tk, ctx_len), lam_init=lam_init)
    return lat, ctx


def _tri(n, upper):
    r = lax.broadcasted_iota(jnp.int32, (n, n), 0)
    c = lax.broadcasted_iota(jnp.int32, (n, n), 1)
    return (c >= r) if upper else (c <= r)


def _mlstm_dir(bwd, q_ref, kt_ref, v_ref, sm_ref, smt_ref, gb_ref, gbt_ref, o_ref, c_ref, m_ref, first):
    cl = q_ref.shape[0]

    @pl.when(first)
    def _():
        c_ref[...] = jnp.zeros_like(c_ref)
        m_ref[...] = jnp.zeros_like(m_ref)

    assert cl == LANES == ML_DV
    allowed = _tri(cl, bwd)
    pre = sm_ref[...] + gb_ref[...]
    pre_t = smt_ref[...] + gbt_ref[...]
    lf = -_softplus(-pre)
    lf_t = -_softplus(-pre_t)
    g_col = _select_dot(lf, allowed, False)
    g_row = _select_dot(lf_t, _tri(cl, not bwd), True)
    q = q_ref[...]
    kt = kt_ref[...]
    lane_q = lax.broadcasted_iota(jnp.int32, q.shape, 1)
    ones = jnp.ones((cl, LANES), BF16)
    end = 0 if bwd else cl - 1
    twice = lambda a: jnp.concatenate([a, a], 1)
    for hh in range(ML_H):
        ci, cf = (2 * bwd) * ML_H + hh, (2 * bwd + 1) * ML_H + hh
        g_c = jnp.broadcast_to(g_col[:, cf:cf + 1], (cl, LANES))
        i_c = jnp.broadcast_to(pre[:, ci:ci + 1], (cl, LANES))
        g_r, i_r = g_row[cf:cf + 1, :], pre_t[ci:ci + 1, :]
        m_prev = m_ref[hh:hh + 1, :]
        qh = jnp.where((lane_q >= hh * ML_DK) & (lane_q < (hh + 1) * ML_DK), q, jnp.zeros_like(q))
        vh = jnp.concatenate([v_ref[:, hh * ML_DV:(hh + 1) * ML_DV], ones], 1)
        logw = jnp.where(allowed, g_c - g_r + i_r, -jnp.inf)
        log_inter = g_c + m_prev
        m_loc = jnp.broadcast_to(jnp.max(logw, -1, keepdims=True), (cl, LANES))
        m_j = jnp.maximum(log_inter, m_loc)
        w = jnp.exp(logw - m_j) * jnp.dot(qh, kt, preferred_element_type=F32)
        a_inter = jnp.exp(log_inter - m_j)
        cstate = c_ref[...]
        nd = _bdot(w, vh) + twice(a_inter) * _bdot(qh, cstate)
        den = jnp.broadcast_to(nd[:, ML_DV:ML_DV + 1], (cl, LANES))
        o_ref[:, hh * ML_DV:(hh + 1) * ML_DV] = nd[:, :ML_DV] / jnp.maximum(jnp.abs(den), jnp.exp(-m_j))
        g_end = g_c[end:end + 1, :]
        log_s = g_end - g_c + i_c
        m_new = jnp.maximum(g_end + m_prev, jnp.max(log_s, 0, keepdims=True))
        ws = jnp.exp(log_s - m_new)
        decay = jnp.exp(g_end + m_prev - m_new)
        rows = slice(hh * ML_DK, (hh + 1) * ML_DK)
        upd = jnp.dot(kt[rows, :], (twice(ws) * vh.astype(F32)).astype(BF16), preferred_element_type=F32)
        c_ref[rows, :] = twice(decay) * cstate[rows, :] + upd
        m_ref[hh:hh + 1, :] = m_new


def _mlstm_kernel(qf, ktf, vf, smf, smtf, qb, ktb, vb, smb, smtb, gb_ref, gbt_ref, of_ref, ob_ref,
                  cf_ref, mf_ref, cb_ref, mb_ref):
    first = pl.program_id(0) == 0
    _mlstm_dir(0, qf, ktf, vf, smf, smtf, gb_ref, gbt_ref, of_ref, cf_ref, mf_ref, first)
    _mlstm_dir(1, qb, ktb, vb, smb, smtb, gb_ref, gbt_ref, ob_ref, cb_ref, mb_ref, first)


def _scan_orders(nc, ncc):
    nlc = nc - ncc
    fwd = lambda c: jnp.where(c < ncc, nlc + c, c - ncc)
    bwd = lambda c: nc - 1 - c
    return fwd, bwd


def _mlstm_scan(q, kt, v, small, small_t, gate_b, ctx_len, cl):
    s = q.shape[0]
    nc, ncc = s // cl, ctx_len // cl
    gb = jnp.zeros((1, LANES), F32).at[0, SM_GATE:SM_GATE + 4 * ML_H].set(gate_b.reshape(-1))

    def specs(order):
        return [pl.BlockSpec((cl, ML_H * ML_DK), lambda c: (order(c), 0)),
                pl.BlockSpec((ML_H * ML_DK, cl), lambda c: (0, order(c))),
                pl.BlockSpec((cl, ML_H * ML_DV), lambda c: (order(c), 0)),
                pl.BlockSpec((cl, LANES), lambda c: (order(c), 0)),
                pl.BlockSpec((LANES, cl), lambda c: (0, order(c)))]

    fwd, bwd = _scan_orders(nc, ncc)
    const = lambda a: pl.BlockSpec(a.shape, lambda c: (0, 0))
    gbt = gb.reshape(LANES, 1)
    state = [pltpu.VMEM((ML_H * ML_DK, 2 * ML_DV), F32), pltpu.VMEM((8, LANES), F32)]
    return pl.pallas_call(
        _mlstm_kernel,
        grid=(nc,),
        in_specs=specs(fwd) + specs(bwd) + [const(gb), const(gbt)],
        out_specs=[pl.BlockSpec((cl, ML_H * ML_DV), lambda c: (fwd(c), 0)),
                   pl.BlockSpec((cl, ML_H * ML_DV), lambda c: (bwd(c), 0))],
        out_shape=[jax.ShapeDtypeStruct((s, ML_H * ML_DV), F32)] * 2,
        scratch_shapes=state + state,
        compiler_params=_cparams(("arbitrary",)),
        name="mlstm_scan",
    )(q, kt, v, small, small_t, q, kt, v, small, small_t, gb, gbt)


def _ssd_dir(bwd, xs_ref, bt_ref, cm_ref, sm_ref, smt_ref, db_ref, dbt_ref, al_ref, alt_ref, ex_ref,
             o_ref, st_ref, first):
    cl = xs_ref.shape[0]

    @pl.when(first)
    def _():
        st_ref[...] = jnp.zeros_like(st_ref)

    allowed = _tri(cl, bwd)
    dt = _softplus(sm_ref[...] + db_ref[...])
    dt_t = _softplus(smt_ref[...] + dbt_ref[...])
    a = dt * (-jnp.exp(al_ref[...]))
    a_t = dt_t * (-jnp.exp(alt_ref[...]))
    s_col = _select_dot(a, allowed, False)
    s_row = _select_dot(a_t, _tri(cl, not bwd), True)
    end = 0 if bwd else cl - 1
    s_end = s_col[end:end + 1, :]
    stack = jnp.concatenate([jnp.exp(s_col), dt, dt * jnp.exp(s_end - s_col), jnp.exp(jnp.broadcast_to(s_end, (8, LANES)))], 0)
    wide = _select_dot(stack, ex_ref[bwd], True)
    es_w, dt_w, wt_w, dec_w = wide[:cl], wide[cl:2 * cl], wide[2 * cl:3 * cl], wide[3 * cl:3 * cl + 1]
    xs = xs_ref[...].astype(F32)
    xdt = (xs * dt_w).astype(BF16)
    cm = cm_ref[...]
    bt = bt_ref[...]
    lane_c = lax.broadcasted_iota(jnp.int32, cm.shape, 1)
    lane_x = lax.broadcasted_iota(jnp.int32, (cl, LANES), 1)
    state = st_ref[...]
    y = es_w * _bdot(cm, state)
    per_pair = SSD_H // (D_INNER // LANES)
    cbs = []
    for grp in range(SSD_G):
        cg = jnp.where((lane_c >= grp * SSD_N) & (lane_c < (grp + 1) * SSD_N), cm, jnp.zeros_like(cm))
        cbs.append(jnp.dot(cg, bt, preferred_element_type=F32))
    for blk in range(D_INNER // LANES):
        cb = cbs[(blk * per_pair) // (SSD_H // SSD_G)]
        xblk = xdt[:, blk * LANES:(blk + 1) * LANES]
        res = []
        for sub in range(per_pair):
            hh = blk * per_pair + sub
            col = SM_DT + bwd * SSD_H + hh
            decay = jnp.exp(jnp.where(allowed, s_col[:, col:col + 1] - s_row[col:col + 1, :], -jnp.inf))
            res.append(_bdot(decay * cb, xblk))
        intra = jnp.where(lane_x < SSD_P, res[0], res[1])
        o_ref[:, blk * LANES:(blk + 1) * LANES] = y[:, blk * LANES:(blk + 1) * LANES] + intra
    upd = jnp.dot(bt, (xs * wt_w).astype(BF16), preferred_element_type=F32)
    r = lax.broadcasted_iota(jnp.int32, upd.shape, 0) // SSD_N
    c = lax.broadcasted_iota(jnp.int32, upd.shape, 1) // (SSD_P * (SSD_H // SSD_G))
    st_ref[...] = jnp.where(r == c, dec_w * state + upd, 0.0)


def _ssd_kernel(xf, btf, cf, smf, smtf, xb, btb, cb, smb, smtb, db_ref, dbt_ref, al_ref, alt_ref, ex_ref,
                of_ref, ob_ref, sf_ref, sb_ref):
    first = pl.program_id(0) == 0
    _ssd_dir(0, xf, btf, cf, smf, smtf, db_ref, dbt_ref, al_ref, alt_ref, ex_ref, of_ref, sf_ref, first)
    _ssd_dir(1, xb, btb, cb, smb, smtb, db_ref, dbt_ref, al_ref, alt_ref, ex_ref, ob_ref, sb_ref, first)


def _ssd_scan(xs, bt, cm, small, small_t, dt_bias, a_log, ctx_len, cl):
    s = xs.shape[0]
    nc, ncc = s // cl, ctx_len // cl
    db = jnp.zeros((1, LANES), F32).at[0, SM_DT:SM_DT + 2 * SSD_H].set(dt_bias.reshape(-1))
    al = jnp.zeros((1, LANES), F32).at[0, SM_DT:SM_DT + 2 * SSD_H].set(a_log.reshape(-1))
    src = jnp.arange(LANES)[None, :, None]
    dst_head = (jnp.arange(D_INNER) // SSD_P)[None, None, :]
    dirs = jnp.arange(2)[:, None, None]
    expand = (src == SM_DT + dirs * SSD_H + dst_head).astype(BF16)

    def specs(order):
        return [pl.BlockSpec((cl, D_INNER), lambda c: (order(c), 0)),
                pl.BlockSpec((SSD_G * SSD_N, cl), lambda c: (0, order(c))),
                pl.BlockSpec((cl, SSD_G * SSD_N), lambda c: (order(c), 0)),
                pl.BlockSpec((cl, LANES), lambda c: (order(c), 0)),
                pl.BlockSpec((LANES, cl), lambda c: (0, order(c)))]

    fwd, bwd = _scan_orders(nc, ncc)
    const = lambda a: pl.BlockSpec(a.shape, lambda c: (0,) * a.ndim)
    dbt, alt = db.reshape(LANES, 1), al.reshape(LANES, 1)
    state = pltpu.VMEM((SSD_G * SSD_N, D_INNER), F32)
    return pl.pallas_call(
        _ssd_kernel,
        grid=(nc,),
        in_specs=specs(fwd) + specs(bwd) + [const(db), const(dbt), const(al), const(alt), const(expand)],
        out_specs=[pl.BlockSpec((cl, D_INNER), lambda c: (fwd(c), 0)),
                   pl.BlockSpec((cl, D_INNER), lambda c: (bwd(c), 0))],
        out_shape=[jax.ShapeDtypeStruct((s, D_INNER), F32)] * 2,
        scratch_shapes=[state, state],
        compiler_params=_cparams(("arbitrary",)),
        name="ssd_scan",
    )(xs, bt, cm, small, small_t, xs, bt, cm, small, small_t, db, dbt, al, alt, expand)


def _merge_kernel(xl_ref, xc_ref, mod_ref, h_ref, a_ref, hf_ref, hb_ref, mlo_ref, mln_ref, m_ref, yf_ref,
                  yb_ref, xs_ref, z_ref, dsk_ref, sdn_ref, wg_ref, bg_ref, wb_ref, wo_ref, g_ref, b_ref, o_ref,
                  *, alpha, nlt):
    hb16 = h_ref[...]
    hm = hf_ref[...] + hb_ref[...]
    og = _sigmoid(mlo_ref[...].astype(F32))
    b_parts = []
    for hh in range(ML_H):
        sl = slice(hh * ML_DV, (hh + 1) * ML_DV)
        b_parts.append(_rms(hm[:, sl], mln_ref[:, sl]) * og[:, sl])
    b_br = jnp.concatenate(b_parts, 1)
    z = z_ref[...].astype(F32)
    ys = (yf_ref[...] + yb_ref[...] + dsk_ref[...] * xs_ref[...].astype(F32)) * (z * _sigmoid(z))
    gw = D_INNER // SSD_G
    s_br = jnp.concatenate([_rms(ys[:, g * gw:(g + 1) * gw], sdn_ref[:, g * gw:(g + 1) * gw])
                            for g in range(SSD_G)], 1)
    branches = (a_ref[...], b_br, m_ref[...], s_br)
    y = None
    for k, br in enumerate(branches):
        gate = _sigmoid(jnp.dot(hb16, wg_ref[k], preferred_element_type=F32) + bg_ref[k:k + 1, :])
        term = gate * _bdot(br, wb_ref[k])
        y = term if y is None else y + term
    y = _bdot(y, wo_ref[...])
    x1 = alpha * _stream_tile(xl_ref, xc_ref, nlt) + mod_ref[2:3, :] * y
    o_ref[...] = _ln(x1) * g_ref[...] + b_ref[...]


def _merge(x_lat, x_ctx, ctx_tile0, mod, nlt, n_tiles, h, a, hf, hb, mlo, mln, m, yf, yb, xs, z, dsk, sdn, wg,
           bg, wb, wo, g, b, tm, alpha):
    d = x_lat.shape[1]
    row = lambda w: pl.BlockSpec((tm, w), lambda i: (i, 0))
    full = lambda arr: pl.BlockSpec(arr.shape, lambda i: (0,) * arr.ndim)
    return pl.pallas_call(
        functools.partial(_merge_kernel, alpha=alpha, nlt=nlt),
        grid=(n_tiles,),
        in_specs=_stream_specs(tm, d, nlt, ctx_tile0) + [
            pl.BlockSpec((None, 8, d), lambda i: (jnp.where(i < nlt, 0, 1), 0, 0)),
            row(d), row(512), row(512), row(512), row(512), full(mln), row(512), row(512), row(512),
            row(512), row(512), full(dsk), full(sdn), full(wg), full(bg), full(wb), full(wo),
            full(g), full(b)],
        out_specs=row(d),
        out_shape=jax.ShapeDtypeStruct((n_tiles * tm, d), F32),
        compiler_params=_cparams(("arbitrary",)),
        name="merge_ln1",
    )(x_lat, x_ctx, mod, h, a, hf, hb, mlo, mln, m, yf, yb, xs, z, dsk, sdn, wg, bg, wb, wo, g, b)


def _mlp_kernel(x_ref, mod_ref, wu_ref, bu_ref, wd_ref, bd_ref, g_ref, b_ref, o_ref, *, alpha):
    x = x_ref[...]
    hm = _ln(x) * (1.0 + mod_ref[4:5, :]) + mod_ref[3:4, :]
    u = jnp.maximum(_bdot(hm, wu_ref[...]) + bu_ref[...], 0.0)
    f = _bdot(u * u, wd_ref[...]) + bd_ref[...]
    x2 = alpha * x + mod_ref[5:6, :] * f
    o_ref[...] = _ln(x2) * g_ref[...] + b_ref[...]


def _mlp(x1, mod, nlt, wu, bu, wd, bd, g, b, tm, alpha):
    s, d = x1.shape
    n_tiles = s // tm
    full = lambda arr: pl.BlockSpec(arr.shape, lambda i: (0,) * arr.ndim)
    return pl.pallas_call(
        functools.partial(_mlp_kernel, alpha=alpha),
        grid=(n_tiles,),
        in_specs=[pl.BlockSpec((tm, d), lambda i: (i, 0)),
                  pl.BlockSpec((None, 8, d), lambda i: (jnp.where(i < nlt, 0, 1), 0, 0)),
                  full(wu), full(bu), full(wd), full(bd), full(g), full(b)],
        out_specs=pl.BlockSpec((tm, d), lambda i: (i, 0)),
        out_shape=jax.ShapeDtypeStruct((n_tiles * tm, d), F32),
        compiler_params=_cparams(("arbitrary",)),
        name="mlp_ln2",
    )(x1, mod, wu, bu, wd, bd, g, b)


def _axis_tables(pos, dsub):
    half = dsub // 2
    inv = ROPE_BASE ** (-jnp.arange(half, dtype=F32) * 2.0 / dsub)
    ang = pos[:, None] * inv[None, :]
    c, s = jnp.cos(ang), jnp.sin(ang)
    z = jnp.zeros_like(s)
    return jnp.concatenate([c, c], -1), jnp.concatenate([-s, z], -1), jnp.concatenate([z, s], -1)


def _rope_tables(t, ctx_len):
    n_rows = t // GRID_W
    row, col = jnp.arange(n_rows, dtype=F32), jnp.arange(GRID_W, dtype=F32)

    def two_axis(dsub):
        r = [jnp.repeat(a, GRID_W, axis=0) for a in _axis_tables(row, dsub)]
        c = [jnp.tile(a, (n_rows, 1)) for a in _axis_tables(col, dsub)]
        return [jnp.concatenate([a, b], -1) for a, b in zip(r, c)]

    cd, nd, pd = [jnp.tile(a, (1, 2)) for a in two_axis(DIFF_HD // 2)]
    cm, nm, pm = two_axis(MLA_ROPE // 2)
    ones = lambda w: jnp.ones((t, w), F32)
    zeros = lambda w: jnp.zeros((t, w), F32)
    pad = LANES - MLA_NOPE - MLA_ROPE
    cm = jnp.concatenate([ones(MLA_NOPE), cm, ones(pad)], -1)
    nm = jnp.concatenate([zeros(MLA_NOPE), nm, zeros(pad)], -1)
    pm = jnp.concatenate([zeros(MLA_NOPE), pm, zeros(pad)], -1)
    lat = jnp.concatenate([cd, nd, pd, cm, nm, pm], -1)
    ident = jnp.concatenate([jnp.ones((ctx_len, LANES), F32), jnp.zeros((ctx_len, 2 * LANES), F32)] * 2, -1)
    return jnp.concatenate([lat, ident], 0)


def _relayout_w_in(w):
    d = w.shape[0]
    o = [0, 512, 1024, 1536, 2048, 2560, 3072, 3088, 3472, 3728, 3760, 4272, 5040, 5056]
    seg = lambda i: w[:, o[i]:o[i + 1]]
    qd, kd, vd, mlqk, mlv, mlo, mlg, cq, ckv, kr, z, xbc, dt = [seg(i) for i in range(13)]
    zc = lambda n: jnp.zeros((d, n), w.dtype)
    small = jnp.concatenate([mlg, dt, zc(SM_KR - 32), kr, zc(LANES - SM_KR - MLA_ROPE)], 1)
    return jnp.concatenate([qd, kd, vd, mlqk, mlv, mlo, z, xbc, cq, ckv, small], 1).astype(BF16)


def _pad_heads(w, heads, width):
    r = w.shape[0]
    w = w.reshape(r, heads, width)
    return jnp.pad(w, ((0, 0), (0, 0), (0, LANES - width))).reshape(r, heads * LANES)


def _pick_tile(n, pref):
    for t in pref:
        if n % t == 0:
            return t
    raise ValueError(f"no tile for {n}")


def kernel(x, c, ctx, c_ctx, w_mod, b_mod, w_in, diff_lambda, diff_subln, ml_conv_w, ml_conv_b, ml_gate_b, ml_norm, mla_q_norm, mla_kv_norm, mla_w_uq, mla_w_ukv, ssd_conv_w, ssd_conv_b, ssd_dt_bias, ssd_a_log, ssd_d, ssd_norm, w_gate, b_gate, w_branch, w_o, ln1_g, ln1_b, w_up, b_up, w_down, b_down, ln2_g, ln2_b):
    assert x.shape[0] == 1 and ctx.shape[0] == 1
    depth = w_in.shape[0]
    t, d = x.shape[1], x.shape[2]
    ctx_len = ctx.shape[1]
    s = ctx_len + t
    tm = _pick_tile(math.gcd(ctx_len, t), (256, 128))
    cl = 128
    tq_diff = _pick_tile(t, (256, 128))
    tq_mla = _pick_tile(t, (512, 256, 128))
    tk = _pick_tile(s, (3328, 1280, 640, 256, 128))
    assert ctx_len % cl == 0 and t % cl == 0 and t % GRID_W == 0 and t % ctx_len == 0
    nlt, n = t // tm, s // tm
    alpha = (2 * depth) ** 0.25

    x_lat, x_ctx, ctx_tile0 = x[0], ctx[0], 0
    cvec = jnp.zeros((8, d), F32).at[0].set(c[0]).at[1].set(c_ctx)
    mod = _modulation(cvec, w_mod, b_mod)
    mod = jnp.pad(mod[:, :2].reshape(depth, 2, 6, d), ((0, 0), (0, 0), (0, 2), (0, 0)))
    rope = _rope_tables(t, ctx_len)

    for l in range(depth):
        ctx_out = l < depth - 1
        lam_init = 0.8 - 0.6 * math.exp(-0.3 * l)
        wq = _pad_heads(mla_w_uq[l], MLA_H, MLA_NOPE + MLA_ROPE).astype(BF16)
        wkv = mla_w_ukv[l].reshape(MLA_KV_RANK, MLA_H, MLA_NOPE + MLA_V)
        wk = _pad_heads(wkv[:, :, :MLA_NOPE].reshape(MLA_KV_RANK, -1), MLA_H, MLA_NOPE).astype(BF16)
        wv = wkv[:, :, MLA_NOPE:].reshape(MLA_KV_RANK, -1).astype(BF16)
        (h, qd, kd, vd, mlqk, mlv, mlo, z, xbc, small, qm, km, vm) = _inproj(
            x_lat, x_ctx, ctx_tile0, s, mod[l], nlt, _relayout_w_in(w_in[l]), rope, mla_q_norm[l][None],
            mla_kv_norm[l][None], wq, wk, wv, tm)

        a_lat, a_ctx = _attention(qd, kd, vd, diff_lambda[l], diff_subln[l][None], diff=True,
                                  ctx_len=ctx_len, ctx_out=ctx_out, tq=tq_diff, tk=tk, lam_init=lam_init)
        m_lat, m_ctx = _attention(qm, km, vm, diff_lambda[l], diff_subln[l][None], diff=False,
                                  ctx_len=ctx_len, ctx_out=ctx_out, tq=tq_mla, tk=tk)

        ml_q, ml_k = _conv_silu(mlqk, ml_conv_w[l], ml_conv_b[l], nlt, tm,
                                (ML_H * ML_DK, ML_H * ML_DK), (1.0, ML_DK ** -0.5))
        xs, bm, cm = _conv_silu(xbc, ssd_conv_w[l], ssd_conv_b[l], nlt, tm,
                                (D_INNER, SSD_G * SSD_N, SSD_G * SSD_N), (1.0, 1.0, 1.0))
        small_t = small.T
        hf, hb = _mlstm_scan(ml_q, ml_k.T, mlv, small, small_t, ml_gate_b[l], ctx_len, cl)
        yf, yb = _ssd_scan(xs, bm.T, cm, small, small_t, ssd_dt_bias[l], ssd_a_log[l], ctx_len, cl)

        if ctx_out:
            a_all = jnp.concatenate([a_lat, a_ctx], 0)
            m_all = jnp.concatenate([m_lat, m_ctx], 0)
            n_tiles = n
        else:
            a_all, m_all = a_lat, m_lat
            n_tiles = nlt
        x1 = _merge(x_lat, x_ctx, ctx_tile0, mod[l], nlt, n_tiles, h, a_all, hf, hb, mlo, ml_norm[l][None],
                    m_all, yf, yb, xs, z, jnp.repeat(ssd_d[l], SSD_P)[None], ssd_norm[l][None],
                    w_gate[l].astype(BF16), b_gate[l], w_branch[l].astype(BF16), w_o[l].astype(BF16),
                    ln1_g[l][None], ln1_b[l][None], tm, alpha)
        x_lat = _mlp(x1, mod[l], nlt, w_up[l].astype(BF16), b_up[l][None],
                     w_down[l].astype(BF16), b_down[l][None], ln2_g[l][None], ln2_b[l][None], tm, alpha)
        x_ctx, ctx_tile0 = x_lat, nlt
    return x_lat[None]
```

```python
import functools
import math

import jax
import jax.numpy as jnp
from jax import lax
from jax.experimental import pallas as pl
from jax.experimental.pallas import tpu as pltpu

F32 = jnp.float32
BF16 = jnp.bfloat16

GRID_W = 64
ROPE_BASE = 10000.0
CONV_K = 5
EPS = 1e-6
LOG2E = math.log2(math.e)
LANES = 128
CONV_HALO = 8

DIFF_H, DIFF_HD = 4, 64
ML_H, ML_DK, ML_DV = 4, 64, 128
MLA_H, MLA_Q_RANK, MLA_KV_RANK, MLA_NOPE, MLA_ROPE, MLA_V = 4, 384, 256, 64, 32, 128
SSD_H, SSD_P, SSD_G, SSD_N = 8, 64, 2, 64
D_INNER = SSD_H * SSD_P

C_QD, C_KD, C_VD, C_MLQK, C_MLV, C_MLO, C_Z, C_XBC, C_CQ, C_CKV, C_SMALL, C_END = (
    0, 512, 1024, 1536, 2048, 2560, 3072, 3584, 4352, 4736, 4992, 5120)
SM_GATE, SM_DT, SM_KR = 0, 16, 64

VMEM_LIMIT = 60 * 1024 * 1024


def _cparams(sem):
    return pltpu.CompilerParams(dimension_semantics=sem, vmem_limit_bytes=VMEM_LIMIT)


def _ln(x):
    mu = jnp.mean(x, -1, keepdims=True)
    xc = x - mu
    var = jnp.mean(xc * xc, -1, keepdims=True)
    return xc * lax.rsqrt(var + EPS)


def _sigmoid(x):
    return 0.5 * jnp.tanh(0.5 * x) + 0.5


def _softplus(x):
    return jnp.maximum(x, 0.0) + jnp.log1p(jnp.exp(-jnp.abs(x)))


def _bdot(a, b):
    return jnp.dot(a.astype(BF16), b.astype(BF16), preferred_element_type=F32)


def _hdot(a, b):
    return jnp.dot(a, b, precision=lax.Precision.HIGHEST, preferred_element_type=F32)


def _select_dot(x, sel, x_is_lhs):
    sel = sel.astype(BF16)
    acc, rest = None, x
    for _ in range(3):
        part = rest.astype(BF16)
        prod = (jnp.dot(part, sel, preferred_element_type=F32) if x_is_lhs
                else jnp.dot(sel, part, preferred_element_type=F32))
        acc = prod if acc is None else acc + prod
        rest = rest - part.astype(F32)
    return acc


def _mod_kernel(c_ref, w_ref, b_ref, o_ref):
    c = c_ref[...]
    o_ref[...] = _hdot(c * _sigmoid(c), w_ref[...]) + b_ref[...]


def _modulation(cvec, w_mod, b_mod):
    depth, d, n = w_mod.shape
    tn = 1536
    return pl.pallas_call(
        _mod_kernel,
        grid=(depth, n // tn),
        in_specs=[pl.BlockSpec((8, d), lambda l, j: (0, 0)),
                  pl.BlockSpec((None, d, tn), lambda l, j: (l, 0, j)),
                  pl.BlockSpec((None, 1, tn), lambda l, j: (l, 0, j))],
        out_specs=pl.BlockSpec((None, 8, tn), lambda l, j: (l, 0, j)),
        out_shape=jax.ShapeDtypeStruct((depth, 8, n), F32),
        compiler_params=_cparams(("arbitrary", "arbitrary")),
        name="modulation",
    )(cvec, w_mod, b_mod.reshape(depth, 1, n))


def _rope(x, cos, sin_next, sin_prev, partner):
    return (x * cos + pltpu.roll(x, LANES - partner, 1) * sin_next
            + pltpu.roll(x, partner, 1) * sin_prev)


def _rms(x, w):
    return x * lax.rsqrt(jnp.mean(x * x, -1, keepdims=True) + EPS) * w


def _inproj_kernel(xl_ref, xc_ref, mod_ref, win_ref, roper_ref, ropec_ref, qn_ref, kvn_ref, wuq_ref, wuk_ref, wuv_ref,
                   h_ref, qd_ref, kd_ref, vd_ref, mlqk_ref, mlv_ref, mlo_ref, z_ref, xbc_ref,
                   small_ref, qm_ref, km_ref, vm_ref, *, nlt):
    x = _stream_tile(xl_ref, xc_ref, nlt)
    h = _ln(x) * (1.0 + mod_ref[1:2, :]) + mod_ref[0:1, :]
    hb = h.astype(BF16)
    h_ref[...] = hb

    def proj(lo, hi):
        return jnp.dot(hb, win_ref[:, lo:hi], preferred_element_type=F32)

    col_part = jnp.where(pl.program_id(0) < nlt, ropec_ref[...], 0.0)
    rope = jnp.concatenate([roper_ref[g] + col_part for g in range(x.shape[0] // GRID_W)], 0)
    cos_d, sn_d, sp_d = rope[:, 0:128], rope[:, 128:256], rope[:, 256:384]
    cos_m, sn_m, sp_m = rope[:, 384:512], rope[:, 512:640], rope[:, 640:768]

    lane = lax.broadcasted_iota(jnp.int32, (x.shape[0], LANES), 1)
    one_col = (lane == 0).astype(BF16)

    qd = proj(C_QD, C_KD)
    kd = proj(C_KD, C_VD)
    vd = proj(C_VD, C_MLQK).astype(BF16)
    for hh in range(DIFF_H):
        sl = slice(hh * LANES, (hh + 1) * LANES)
        qd_ref[:, sl] = (_rope(qd[:, sl], cos_d, sn_d, sp_d, 16) * (DIFF_HD ** -0.5 * LOG2E)).astype(BF16)
        kd_ref[:, sl] = _rope(kd[:, sl], cos_d, sn_d, sp_d, 16).astype(BF16)
        vd_ref[:, 2 * hh * LANES:(2 * hh + 1) * LANES] = vd[:, sl]
        vd_ref[:, (2 * hh + 1) * LANES:(2 * hh + 2) * LANES] = one_col
    mlqk_ref[...] = proj(C_MLQK, C_MLV)
    mlv_ref[...] = proj(C_MLV, C_MLO).astype(BF16)
    mlo_ref[...] = proj(C_MLO, C_Z).astype(BF16)
    z_ref[...] = proj(C_Z, C_XBC).astype(BF16)
    xbc_ref[...] = proj(C_XBC, C_CQ)
    small = proj(C_SMALL, C_END)
    small_ref[...] = small

    cq = _rms(proj(C_CQ, C_CKV), qn_ref[...])
    ckv = _rms(proj(C_CKV, C_SMALL), kvn_ref[...])
    qm = _bdot(cq, wuq_ref[...])
    km = _bdot(ckv, wuk_ref[...])
    vm = _bdot(ckv, wuv_ref[...]).astype(BF16)
    kr = jnp.where((lane >= SM_KR) & (lane < SM_KR + MLA_ROPE), small, 0.0)
    kr = _rope(kr, cos_m, sn_m, sp_m, 8)
    scale = (MLA_NOPE + MLA_ROPE) ** -0.5 * LOG2E
    for hh in range(MLA_H):
        sl = slice(hh * LANES, (hh + 1) * LANES)
        qm_ref[:, sl] = (_rope(qm[:, sl], cos_m, sn_m, sp_m, 8) * scale).astype(BF16)
        km_ref[:, sl] = (km[:, sl] + kr).astype(BF16)
        vm_ref[:, 2 * hh * LANES:(2 * hh + 1) * LANES] = vm[:, sl]
        vm_ref[:, (2 * hh + 1) * LANES:(2 * hh + 2) * LANES] = one_col


def _stream_specs(tm, d, nlt, ctx_tile0):
    return [pl.BlockSpec((tm, d), lambda i: (jnp.minimum(i, nlt - 1), 0)),
            pl.BlockSpec((tm, d), lambda i: (jnp.maximum(i - nlt, 0) + ctx_tile0, 0))]


def _stream_tile(xl_ref, xc_ref, nlt):
    return jnp.where(pl.program_id(0) < nlt, xl_ref[...], xc_ref[...])


def _inproj(x_lat, x_ctx, ctx_tile0, s, mod, nlt, w_in, rope, qn, kvn, wuq, wuk, wuv, tm):
    d = x_lat.shape[1]
    n = s // tm
    rope_rows, rope_cols = rope
    row = lambda w: pl.BlockSpec((tm, w), lambda i: (i, 0))
    full = lambda a: pl.BlockSpec(a.shape, lambda i: (0,) * a.ndim)
    outs = [(d, BF16), (512, BF16), (512, BF16), (1024, BF16), (512, F32), (512, BF16), (512, BF16),
            (512, BF16), (768, F32), (128, F32), (512, BF16), (512, BF16), (1024, BF16)]
    return pl.pallas_call(
        functools.partial(_inproj_kernel, nlt=nlt),
        grid=(n,),
        in_specs=_stream_specs(tm, d, nlt, ctx_tile0) + [
            pl.BlockSpec((None, 8, d), lambda i: (jnp.where(i < nlt, 0, 1), 0, 0)),
            full(w_in), pl.BlockSpec((tm // GRID_W, 1, rope_rows.shape[2]), lambda i: (i, 0, 0)),
            full(rope_cols), full(qn), full(kvn), full(wuq), full(wuk), full(wuv)],
        out_specs=[row(w) for w, _ in outs],
        out_shape=[jax.ShapeDtypeStruct((s, w), dt) for w, dt in outs],
        compiler_params=_cparams(("arbitrary",)),
        name="inproj",
    )(x_lat, x_ctx, mod, w_in, rope_rows, rope_cols, qn, kvn, wuq, wuk, wuv)


def _conv_kernel(cur_ref, prev_ref, next_ref, w_ref, b_ref, *rest, splits, scales, nlt, n):
    out_refs, ext_ref = rest[:-1], rest[-1]
    i = pl.program_id(0)
    tm = cur_ref.shape[0]
    has_prev = jnp.logical_and(i != 0, i != nlt).astype(F32)
    has_next = jnp.logical_and(i != nlt - 1, i != n - 1).astype(F32)
    ext_ref[0:CONV_HALO, :] = prev_ref[...] * has_prev
    ext_ref[CONV_HALO:CONV_HALO + tm, :] = cur_ref[...]
    ext_ref[CONV_HALO + tm:, :] = next_ref[...] * has_next
    acc = b_ref[...] + w_ref[0:1, :] * ext_ref[pl.ds(CONV_HALO - 2, tm), :]
    for k in range(1, CONV_K):
        acc = acc + w_ref[k:k + 1, :] * ext_ref[pl.ds(CONV_HALO - 2 + k, tm), :]
    y = acc * _sigmoid(acc)
    lo = 0
    for ref, wd, sc in zip(out_refs, splits, scales):
        ref[...] = (y[:, lo:lo + wd] * sc).astype(ref.dtype)
        lo += wd


def _conv_silu(xin, w, b, nlt, tm, splits, scales):
    s, c = xin.shape
    n = s // tm
    r = tm // CONV_HALO
    wpad = jnp.zeros((8, c), F32).at[:CONV_K].set(w)
    return pl.pallas_call(
        functools.partial(_conv_kernel, splits=splits, scales=scales, nlt=nlt, n=n),
        grid=(n,),
        in_specs=[pl.BlockSpec((tm, c), lambda i: (i, 0)),
                  pl.BlockSpec((CONV_HALO, c), lambda i: (jnp.maximum(i * r - 1, 0), 0)),
                  pl.BlockSpec((CONV_HALO, c), lambda i: (jnp.minimum((i + 1) * r, n * r - 1), 0)),
                  pl.BlockSpec((8, c), lambda i: (0, 0)),
                  pl.BlockSpec((1, c), lambda i: (0, 0))],
        out_specs=[pl.BlockSpec((tm, wd), lambda i: (i, 0)) for wd in splits],
        out_shape=[jax.ShapeDtypeStruct((s, wd), BF16) for wd in splits],
        scratch_shapes=[pltpu.VMEM((tm + 2 * CONV_HALO, c), F32)],
        compiler_params=_cparams(("arbitrary",)),
        name="conv_silu",
    )(xin, xin, xin, wpad, b.reshape(1, c))


def _flash_kernel(q_ref, k_ref, v_ref, lam_ref, sub_ref, o_ref, qs_ref, s_buf, p_buf, a_buf, mx_buf, m_ref, acc_ref,
                  *, diff, tk, nkv, lam_init):
    tq = q_ref.shape[0]
    if diff:
        q = q_ref[...]
        lane = lax.broadcasted_iota(jnp.int32, q.shape, 1)
        zero = jnp.zeros_like(q)
        qs_ref[0:tq, :] = jnp.where(lane < DIFF_HD, q, zero)
        qs_ref[tq:, :] = jnp.where(lane >= DIFF_HD, q, zero)
    else:
        qs_ref[...] = q_ref[...]
    m_ref[...] = jnp.full(m_ref.shape, -jnp.inf, F32)
    acc_ref[...] = jnp.zeros_like(acc_ref)

    def scores(j, slot):
        off = pl.multiple_of(j * tk, tk)
        s_buf[slot] = lax.dot_general(qs_ref[...], k_ref[pl.ds(off, tk), :], (((1,), (1,)), ((), ())),
                                      preferred_element_type=F32)

    def softmax(slot):
        s = s_buf[slot]
        m_old = m_ref[...]
        m_new = jnp.maximum(m_old, jnp.max(s, -1, keepdims=True))
        p_buf[slot] = jnp.exp2(s - m_new).astype(BF16)
        a_buf[slot] = jnp.exp2(m_old - m_new)
        m_ref[...] = m_new

    def values(j, slot):
        off = pl.multiple_of(j * tk, tk)
        acc_ref[...] = a_buf[slot] * acc_ref[...] + jnp.dot(p_buf[slot], v_ref[pl.ds(off, tk), :],
                                                            preferred_element_type=F32)

    scores(0, 0)
    if nkv > 1:
        scores(1, 1)
    softmax(0)

    def step(j, slot):
        scores(j + 1, 1 - slot)
        softmax(slot)
        values(j - 1, 1 - slot)

    def body(jj, carry):
        step(2 * jj + 1, 1)
        step(2 * jj + 2, 0)
        return carry

    pairs = max(nkv - 2, 0) // 2
    if pairs:
        lax.fori_loop(0, pairs, body, 0, unroll=True)
    if max(nkv - 2, 0) % 2:
        step(nkv - 2, (nkv - 2) % 2)
    if nkv > 1:
        softmax((nkv - 1) % 2)
        values(nkv - 2, nkv % 2)
    values(nkv - 1, (nkv - 1) % 2)

    acc = acc_ref[...]
    o = acc[:, :LANES] / acc[:, LANES:LANES + 1]
    if diff:
        lam = lam_ref[...]
        lam_full = (jnp.exp(jnp.sum(lam[0:1, :] * lam[1:2, :], -1, keepdims=True))
                    - jnp.exp(jnp.sum(lam[2:3, :] * lam[3:4, :], -1, keepdims=True)) + lam_init)
        o = o[:tq] - lam_full * o[tq:]
        o = _rms(o, sub_ref[...]) * (1.0 - lam_init)
    o_ref[...] = o.astype(o_ref.dtype)


def _flash(q, k, v, lam, subln, *, diff, q_tile0, nq, kv_block, kv_len, tq, tk, lam_init=0.0):
    heads = q.shape[1] // LANES
    rows = 2 * tq if diff else tq
    return pl.pallas_call(
        functools.partial(_flash_kernel, diff=diff, tk=tk, nkv=kv_len // tk, lam_init=lam_init),
        grid=(heads, nq),
        in_specs=[pl.BlockSpec((tq, LANES), lambda h, i: (i + q_tile0, h)),
                  pl.BlockSpec((kv_len, LANES), lambda h, i: (kv_block, h)),
                  pl.BlockSpec((kv_len, 2 * LANES), lambda h, i: (kv_block, h)),
                  pl.BlockSpec(lam.shape, lambda h, i: (0, 0)),
                  pl.BlockSpec(subln.shape, lambda h, i: (0, 0))],
        out_specs=pl.BlockSpec((tq, LANES), lambda h, i: (i, h)),
        out_shape=jax.ShapeDtypeStruct((nq * tq, heads * LANES), BF16),
        scratch_shapes=[pltpu.VMEM((rows, LANES), BF16),
                        pltpu.VMEM((2, rows, tk), F32),
                        pltpu.VMEM((2, rows, tk), BF16),
                        pltpu.VMEM((2, rows, 1), F32),
                        pltpu.VMEM((2, rows, 1), F32),
                        pltpu.VMEM((rows, 1), F32),
                        pltpu.VMEM((rows, 2 * LANES), F32)],
        compiler_params=_cparams(("arbitrary", "arbitrary")),
        name="flash_diff" if diff else "flash_mla",
    )(q, k, v, lam, subln)


def _attention(q, k, v, lam, subln, *, diff, ctx_len, ctx_out, tq, tk, lam_init=0.0):
    s = q.shape[0]
    t = s - ctx_len
    lat = _flash(q, k, v, lam, subln, diff=diff, q_tile0=0, nq=t // tq, kv_block=0, kv_len=s,
                 tq=tq, tk=tk, lam_init=lam_init)
    if not ctx_out:
        return lat, None
    tqc = math.gcd(tq, ctx_len)
    ctx = _flash(q, k, v, lam, subln, diff=diff, q_tile0=t // tqc, nq=ctx_len // tqc, kv_block=t // ctx_len,
                 kv_len=ctx_len, tq=tqc, tk=math.gcd(tk, ctx_len), lam_init=lam_init)
    return lat, ctx


def _tri(n, upper):
    r = lax.broadcasted_iota(jnp.int32, (n, n), 0)
    c = lax.broadcasted_iota(jnp.int32, (n, n), 1)
    return (c >= r) if upper else (c <= r)


def _mlstm_dir(bwd, q_ref, kt_ref, v_ref, sm_ref, smt_ref, gb_ref, gbt_ref, o_ref, c_ref, m_ref):
    cl = q_ref.shape[0]
    assert cl == LANES == ML_DV
    allowed = _tri(cl, bwd)
    pre = sm_ref[...] + gb_ref[...]
    pre_t = smt_ref[...] + gbt_ref[...]
    lf = -_softplus(-pre)
    lf_t = -_softplus(-pre_t)
    g_col = _select_dot(lf, allowed, False)
    g_row = _select_dot(lf_t, _tri(cl, not bwd), True)
    q = q_ref[...]
    kt = kt_ref[...]
    lane_q = lax.broadcasted_iota(jnp.int32, q.shape, 1)
    ones = jnp.ones((cl, LANES), BF16)
    end = 0 if bwd else cl - 1
    twice = lambda a: jnp.concatenate([a, a], 1)
    for hh in range(ML_H):
        ci, cf = (2 * bwd) * ML_H + hh, (2 * bwd + 1) * ML_H + hh
        g_c = jnp.broadcast_to(g_col[:, cf:cf + 1], (cl, LANES))
        i_c = jnp.broadcast_to(pre[:, ci:ci + 1], (cl, LANES))
        g_r, i_r = g_row[cf:cf + 1, :], pre_t[ci:ci + 1, :]
        m_prev = m_ref[hh:hh + 1, :]
        qh = jnp.where((lane_q >= hh * ML_DK) & (lane_q < (hh + 1) * ML_DK), q, jnp.zeros_like(q))
        vh = jnp.concatenate([v_ref[:, hh * ML_DV:(hh + 1) * ML_DV], ones], 1)
        logw = jnp.where(allowed, g_c - g_r + i_r, -jnp.inf)
        log_inter = g_c + m_prev
        m_loc = jnp.broadcast_to(jnp.max(logw, -1, keepdims=True), (cl, LANES))
        m_j = jnp.maximum(log_inter, m_loc)
        w = jnp.exp(logw - m_j) * jnp.dot(qh, kt, preferred_element_type=F32)
        a_inter = jnp.exp(log_inter - m_j)
        cstate = c_ref[...]
        nd = _bdot(w, vh) + twice(a_inter) * _bdot(qh, cstate)
        den = jnp.broadcast_to(nd[:, ML_DV:ML_DV + 1], (cl, LANES))
        o_ref[:, hh * ML_DV:(hh + 1) * ML_DV] = nd[:, :ML_DV] / jnp.maximum(jnp.abs(den), jnp.exp(-m_j))
        g_end = g_c[end:end + 1, :]
        log_s = g_end - g_c + i_c
        m_new = jnp.maximum(g_end + m_prev, jnp.max(log_s, 0, keepdims=True))
        ws = jnp.exp(log_s - m_new)
        decay = jnp.exp(g_end + m_prev - m_new)
        rows = slice(hh * ML_DK, (hh + 1) * ML_DK)
        upd = jnp.dot(kt[rows, :], (twice(ws) * vh.astype(F32)).astype(BF16), preferred_element_type=F32)
        c_ref[rows, :] = twice(decay) * cstate[rows, :] + upd
        m_ref[hh:hh + 1, :] = m_new


def _ssd_dir(bwd, xs_ref, bt_ref, cm_ref, sm_ref, smt_ref, db_ref, dbt_ref, al_ref, alt_ref, ex_ref,
             o_ref, st_ref):
    cl = xs_ref.shape[0]
    allowed = _tri(cl, bwd)
    dt = _softplus(sm_ref[...] + db_ref[...])
    dt_t = _softplus(smt_ref[...] + dbt_ref[...])
    a = dt * (-jnp.exp(al_ref[...]))
    a_t = dt_t * (-jnp.exp(alt_ref[...]))
    s_col = _select_dot(a, allowed, False)
    s_row = _select_dot(a_t, _tri(cl, not bwd), True)
    end = 0 if bwd else cl - 1
    s_end = s_col[end:end + 1, :]
    stack = jnp.concatenate([jnp.exp(s_col), dt, dt * jnp.exp(s_end - s_col), jnp.exp(jnp.broadcast_to(s_end, (8, LANES)))], 0)
    wide = _select_dot(stack, ex_ref[bwd], True)
    es_w, dt_w, wt_w, dec_w = wide[:cl], wide[cl:2 * cl], wide[2 * cl:3 * cl], wide[3 * cl:3 * cl + 1]
    xs = xs_ref[...].astype(F32)
    xdt = (xs * dt_w).astype(BF16)
    cm = cm_ref[...]
    bt = bt_ref[...]
    lane_c = lax.broadcasted_iota(jnp.int32, cm.shape, 1)
    lane_x = lax.broadcasted_iota(jnp.int32, (cl, LANES), 1)
    state = st_ref[...]
    y = es_w * _bdot(cm, state)
    per_pair = SSD_H // (D_INNER // LANES)
    cbs = []
    for grp in range(SSD_G):
        cg = jnp.where((lane_c >= grp * SSD_N) & (lane_c < (grp + 1) * SSD_N), cm, jnp.zeros_like(cm))
        cbs.append(jnp.dot(cg, bt, preferred_element_type=F32))
    for blk in range(D_INNER // LANES):
        cb = cbs[(blk * per_pair) // (SSD_H // SSD_G)]
        xblk = xdt[:, blk * LANES:(blk + 1) * LANES]
        res = []
        for sub in range(per_pair):
            hh = blk * per_pair + sub
            col = SM_DT + bwd * SSD_H + hh
            decay = jnp.exp(jnp.where(allowed, s_col[:, col:col + 1] - s_row[col:col + 1, :], -jnp.inf))
            res.append(_bdot(decay * cb, xblk))
        intra = jnp.where(lane_x < SSD_P, res[0], res[1])
        o_ref[:, blk * LANES:(blk + 1) * LANES] = y[:, blk * LANES:(blk + 1) * LANES] + intra
    upd = jnp.dot(bt, (xs * wt_w).astype(BF16), preferred_element_type=F32)
    r = lax.broadcasted_iota(jnp.int32, upd.shape, 0) // SSD_N
    c = lax.broadcasted_iota(jnp.int32, upd.shape, 1) // (SSD_P * (SSD_H // SSD_G))
    st_ref[...] = jnp.where(r == c, dec_w * state + upd, 0.0)


def _scan_kernel(qf, ktf, vf, xf, btf, cmf, smf, smtf, qb, ktb, vb, xb, btb, cmb, smb, smtb,
                 gb_ref, gbt_ref, db_ref, dbt_ref, al_ref, alt_ref, ex_ref,
                 hf_ref, hb_ref, yf_ref, yb_ref, cf_ref, mf_ref, cb_ref, mb_ref, sf_ref, sb_ref):
    @pl.when(pl.program_id(0) == 0)
    def _():
        for ref in (cf_ref, mf_ref, cb_ref, mb_ref, sf_ref, sb_ref):
            ref[...] = jnp.zeros_like(ref)

    _mlstm_dir(0, qf, ktf, vf, smf, smtf, gb_ref, gbt_ref, hf_ref, cf_ref, mf_ref)
    _ssd_dir(0, xf, btf, cmf, smf, smtf, db_ref, dbt_ref, al_ref, alt_ref, ex_ref, yf_ref, sf_ref)
    _mlstm_dir(1, qb, ktb, vb, smb, smtb, gb_ref, gbt_ref, hb_ref, cb_ref, mb_ref)
    _ssd_dir(1, xb, btb, cmb, smb, smtb, db_ref, dbt_ref, al_ref, alt_ref, ex_ref, yb_ref, sb_ref)


def _scan_orders(nc, ncc):
    nlc = nc - ncc
    fwd = lambda c: jnp.where(c < ncc, nlc + c, c - ncc)
    bwd = lambda c: nc - 1 - c
    return fwd, bwd


def _scans(q, kt, v, xs, bt, cm, small, small_t, gate_b, dt_bias, a_log, ctx_len, cl):
    s = xs.shape[0]
    nc, ncc = s // cl, ctx_len // cl
    gb = jnp.zeros((1, LANES), F32).at[0, SM_GATE:SM_GATE + 4 * ML_H].set(gate_b.reshape(-1))
    db = jnp.zeros((1, LANES), F32).at[0, SM_DT:SM_DT + 2 * SSD_H].set(dt_bias.reshape(-1))
    al = jnp.zeros((1, LANES), F32).at[0, SM_DT:SM_DT + 2 * SSD_H].set(a_log.reshape(-1))
    src = jnp.arange(LANES)[None, :, None]
    dst_head = (jnp.arange(D_INNER) // SSD_P)[None, None, :]
    dirs = jnp.arange(2)[:, None, None]
    expand = (src == SM_DT + dirs * SSD_H + dst_head).astype(BF16)

    def specs(order):
        rows = lambda w: pl.BlockSpec((cl, w), lambda c: (order(c), 0))
        cols = lambda h: pl.BlockSpec((h, cl), lambda c: (0, order(c)))
        return [rows(ML_H * ML_DK), cols(ML_H * ML_DK), rows(ML_H * ML_DV),
                rows(D_INNER), cols(SSD_G * SSD_N), rows(SSD_G * SSD_N), rows(LANES), cols(LANES)]

    fwd, bwd = _scan_orders(nc, ncc)
    const = lambda a: pl.BlockSpec(a.shape, lambda c: (0,) * a.ndim)
    gbt, dbt, alt = gb.reshape(LANES, 1), db.reshape(LANES, 1), al.reshape(LANES, 1)
    consts = [gb, gbt, db, dbt, al, alt, expand]
    ml_state = [pltpu.VMEM((ML_H * ML_DK, 2 * ML_DV), F32), pltpu.VMEM((8, LANES), F32)]
    ssd_state = pltpu.VMEM((SSD_G * SSD_N, D_INNER), F32)
    out = lambda order, w: pl.BlockSpec((cl, w), lambda c: (order(c), 0))
    operands = (q, kt, v, xs, bt, cm, small, small_t)
    return pl.pallas_call(
        _scan_kernel,
        grid=(nc,),
        in_specs=specs(fwd) + specs(bwd) + [const(a) for a in consts],
        out_specs=[out(fwd, ML_H * ML_DV), out(bwd, ML_H * ML_DV), out(fwd, D_INNER), out(bwd, D_INNER)],
        out_shape=[jax.ShapeDtypeStruct((s, ML_H * ML_DV), F32)] * 2
        + [jax.ShapeDtypeStruct((s, D_INNER), F32)] * 2,
        scratch_shapes=ml_state + ml_state + [ssd_state, ssd_state],
        compiler_params=_cparams(("arbitrary",)),
        name="scans",
    )(*operands, *operands, *consts)


def _merge_kernel(xl_ref, xc_ref, mod_ref, h_ref, a_ref, hf_ref, hb_ref, mlo_ref, mln_ref, m_ref, yf_ref,
                  yb_ref, xs_ref, z_ref, dsk_ref, sdn_ref, wg_ref, bg_ref, wb_ref, wo_ref, g_ref, b_ref, o_ref,
                  *, alpha, nlt):
    hb16 = h_ref[...]
    hm = hf_ref[...] + hb_ref[...]
    og = _sigmoid(mlo_ref[...].astype(F32))
    b_parts = []
    for hh in range(ML_H):
        sl = slice(hh * ML_DV, (hh + 1) * ML_DV)
        b_parts.append(_rms(hm[:, sl], mln_ref[:, sl]) * og[:, sl])
    b_br = jnp.concatenate(b_parts, 1)
    z = z_ref[...].astype(F32)
    ys = (yf_ref[...] + yb_ref[...] + dsk_ref[...] * xs_ref[...].astype(F32)) * (z * _sigmoid(z))
    gw = D_INNER // SSD_G
    s_br = jnp.concatenate([_rms(ys[:, g * gw:(g + 1) * gw], sdn_ref[:, g * gw:(g + 1) * gw])
                            for g in range(SSD_G)], 1)
    branches = (a_ref[...], b_br, m_ref[...], s_br)
    y = None
    for k, br in enumerate(branches):
        gate = _sigmoid(jnp.dot(hb16, wg_ref[k], preferred_element_type=F32) + bg_ref[k:k + 1, :])
        term = gate * _bdot(br, wb_ref[k])
        y = term if y is None else y + term
    y = _bdot(y, wo_ref[...])
    x1 = alpha * _stream_tile(xl_ref, xc_ref, nlt) + mod_ref[2:3, :] * y
    o_ref[...] = _ln(x1) * g_ref[...] + b_ref[...]


def _merge(x_lat, x_ctx, ctx_tile0, mod, nlt, n_tiles, h, a, hf, hb, mlo, mln, m, yf, yb, xs, z, dsk, sdn, wg,
           bg, wb, wo, g, b, tm, alpha):
    d = x_lat.shape[1]
    row = lambda w: pl.BlockSpec((tm, w), lambda i: (i, 0))
    full = lambda arr: pl.BlockSpec(arr.shape, lambda i: (0,) * arr.ndim)
    return pl.pallas_call(
        functools.partial(_merge_kernel, alpha=alpha, nlt=nlt),
        grid=(n_tiles,),
        in_specs=_stream_specs(tm, d, nlt, ctx_tile0) + [
            pl.BlockSpec((None, 8, d), lambda i: (jnp.where(i < nlt, 0, 1), 0, 0)),
            row(d), row(512), row(512), row(512), row(512), full(mln), row(512), row(512), row(512),
            row(512), row(512), full(dsk), full(sdn), full(wg), full(bg), full(wb), full(wo),
            full(g), full(b)],
        out_specs=row(d),
        out_shape=jax.ShapeDtypeStruct((n_tiles * tm, d), F32),
        compiler_params=_cparams(("arbitrary",)),
        name="merge_ln1",
    )(x_lat, x_ctx, mod, h, a, hf, hb, mlo, mln, m, yf, yb, xs, z, dsk, sdn, wg, bg, wb, wo, g, b)


def _mlp_kernel(x_ref, mod_ref, wu_ref, bu_ref, wd_ref, bd_ref, g_ref, b_ref, o_ref, *, alpha):
    x = x_ref[...]
    hm = _ln(x) * (1.0 + mod_ref[4:5, :]) + mod_ref[3:4, :]
    u = jnp.maximum(_bdot(hm, wu_ref[...]) + bu_ref[...], 0.0)
    f = _bdot(u * u, wd_ref[...]) + bd_ref[...]
    x2 = alpha * x + mod_ref[5:6, :] * f
    o_ref[...] = _ln(x2) * g_ref[...] + b_ref[...]


def _mlp(x1, mod, nlt, wu, bu, wd, bd, g, b, tm, alpha):
    s, d = x1.shape
    n_tiles = s // tm
    full = lambda arr: pl.BlockSpec(arr.shape, lambda i: (0,) * arr.ndim)
    return pl.pallas_call(
        functools.partial(_mlp_kernel, alpha=alpha),
        grid=(n_tiles,),
        in_specs=[pl.BlockSpec((tm, d), lambda i: (i, 0)),
                  pl.BlockSpec((None, 8, d), lambda i: (jnp.where(i < nlt, 0, 1), 0, 0)),
                  full(wu), full(bu), full(wd), full(bd), full(g), full(b)],
        out_specs=pl.BlockSpec((tm, d), lambda i: (i, 0)),
        out_shape=jax.ShapeDtypeStruct((n_tiles * tm, d), F32),
        compiler_params=_cparams(("arbitrary",)),
        name="mlp_ln2",
    )(x1, mod, wu, bu, wd, bd, g, b)


def _axis_tables(pos, dsub):
    half = dsub // 2
    inv = ROPE_BASE ** (-jnp.arange(half, dtype=F32) * 2.0 / dsub)
    ang = pos[:, None] * inv[None, :]
    c, s = jnp.cos(ang), jnp.sin(ang)
    z = jnp.zeros_like(s)
    return jnp.concatenate([c, c], -1), jnp.concatenate([-s, z], -1), jnp.concatenate([z, s], -1)


def _rope_tables(t, ctx_len):
    n_rows = t // GRID_W
    pad = LANES - MLA_NOPE - MLA_ROPE

    def parts(pos, is_row):
        n = pos.shape[0]
        zeros = lambda w: jnp.zeros((n, w), F32)
        fill = lambda w, cos: jnp.ones((n, w), F32) if (cos and is_row) else zeros(w)
        out = []
        for k, a in enumerate(_axis_tables(pos, DIFF_HD // 2)):
            one_map = jnp.concatenate([a, zeros(DIFF_HD // 2)] if is_row else [zeros(DIFF_HD // 2), a], -1)
            out.append(jnp.tile(one_map, (1, 2)))
        for k, a in enumerate(_axis_tables(pos, MLA_ROPE // 2)):
            rot = jnp.concatenate([a, zeros(MLA_ROPE // 2)] if is_row else [zeros(MLA_ROPE // 2), a], -1)
            out.append(jnp.concatenate([fill(MLA_NOPE, k == 0), rot, fill(pad, k == 0)], -1))
        return jnp.concatenate(out, -1)

    rows = parts(jnp.arange(n_rows, dtype=F32), True)
    cols = parts(jnp.arange(GRID_W, dtype=F32), False)
    n_ctx = ctx_len // GRID_W
    ident = jnp.concatenate([jnp.ones((n_ctx, LANES), F32), jnp.zeros((n_ctx, 2 * LANES), F32)] * 2, -1)
    return jnp.concatenate([rows, ident], 0)[:, None, :], cols


def _relayout_w_in(w):
    d = w.shape[0]
    o = [0, 512, 1024, 1536, 2048, 2560, 3072, 3088, 3472, 3728, 3760, 4272, 5040, 5056]
    seg = lambda i: w[:, o[i]:o[i + 1]]
    qd, kd, vd, mlqk, mlv, mlo, mlg, cq, ckv, kr, z, xbc, dt = [seg(i) for i in range(13)]
    zc = lambda n: jnp.zeros((d, n), w.dtype)
    small = jnp.concatenate([mlg, dt, zc(SM_KR - 32), kr, zc(LANES - SM_KR - MLA_ROPE)], 1)
    return jnp.concatenate([qd, kd, vd, mlqk, mlv, mlo, z, xbc, cq, ckv, small], 1).astype(BF16)


def _pad_heads(w, heads, width):
    r = w.shape[0]
    w = w.reshape(r, heads, width)
    return jnp.pad(w, ((0, 0), (0, 0), (0, LANES - width))).reshape(r, heads * LANES)


def _pick_tile(n, pref):
    for t in pref:
        if n % t == 0:
            return t
    raise ValueError(f"no tile for {n}")


def kernel(x, c, ctx, c_ctx, w_mod, b_mod, w_in, diff_lambda, diff_subln, ml_conv_w, ml_conv_b, ml_gate_b, ml_norm, mla_q_norm, mla_kv_norm, mla_w_uq, mla_w_ukv, ssd_conv_w, ssd_conv_b, ssd_dt_bias, ssd_a_log, ssd_d, ssd_norm, w_gate, b_gate, w_branch, w_o, ln1_g, ln1_b, w_up, b_up, w_down, b_down, ln2_g, ln2_b):
    assert x.shape[0] == 1 and ctx.shape[0] == 1
    depth = w_in.shape[0]
    t, d = x.shape[1], x.shape[2]
    ctx_len = ctx.shape[1]
    s = ctx_len + t
    tm = _pick_tile(math.gcd(ctx_len, t), (256, 128))
    cl = 128
    tq_diff = _pick_tile(t, (256, 128))
    tq_mla = _pick_tile(t, (512, 256, 128))
    tk = _pick_tile(s, (3328, 1280, 640, 256, 128))
    assert ctx_len % cl == 0 and t % cl == 0 and t % GRID_W == 0 and t % ctx_len == 0
    nlt, n = t // tm, s // tm
    alpha = (2 * depth) ** 0.25

    x_lat, x_ctx, ctx_tile0 = x[0], ctx[0], 0
    cvec = jnp.zeros((8, d), F32).at[0].set(c[0]).at[1].set(c_ctx)
    mod = _modulation(cvec, w_mod, b_mod)
    mod = jnp.pad(mod[:, :2].reshape(depth, 2, 6, d), ((0, 0), (0, 0), (0, 2), (0, 0)))
    rope = _rope_tables(t, ctx_len)

    for l in range(depth):
        ctx_out = l < depth - 1
        lam_init = 0.8 - 0.6 * math.exp(-0.3 * l)
        wq = _pad_heads(mla_w_uq[l], MLA_H, MLA_NOPE + MLA_ROPE).astype(BF16)
        wkv = mla_w_ukv[l].reshape(MLA_KV_RANK, MLA_H, MLA_NOPE + MLA_V)
        wk = _pad_heads(wkv[:, :, :MLA_NOPE].reshape(MLA_KV_RANK, -1), MLA_H, MLA_NOPE).astype(BF16)
        wv = wkv[:, :, MLA_NOPE:].reshape(MLA_KV_RANK, -1).astype(BF16)
        (h, qd, kd, vd, mlqk, mlv, mlo, z, xbc, small, qm, km, vm) = _inproj(
            x_lat, x_ctx, ctx_tile0, s, mod[l], nlt, _relayout_w_in(w_in[l]), rope, mla_q_norm[l][None],
            mla_kv_norm[l][None], wq, wk, wv, tm)

        a_lat, a_ctx = _attention(qd, kd, vd, diff_lambda[l], diff_subln[l][None], diff=True,
                                  ctx_len=ctx_len, ctx_out=ctx_out, tq=tq_diff, tk=tk, lam_init=lam_init)
        m_lat, m_ctx = _attention(qm, km, vm, diff_lambda[l], diff_subln[l][None], diff=False,
                                  ctx_len=ctx_len, ctx_out=ctx_out, tq=tq_mla, tk=tk)

        ml_q, ml_k = _conv_silu(mlqk, ml_conv_w[l], ml_conv_b[l], nlt, tm,
                                (ML_H * ML_DK, ML_H * ML_DK), (1.0, ML_DK ** -0.5))
        xs, bm, cm = _conv_silu(xbc, ssd_conv_w[l], ssd_conv_b[l], nlt, tm,
                                (D_INNER, SSD_G * SSD_N, SSD_G * SSD_N), (1.0, 1.0, 1.0))
        small_t = small.T
        hf, hb, yf, yb = _scans(ml_q, ml_k.T, mlv, xs, bm.T, cm, small, small_t, ml_gate_b[l],
                                ssd_dt_bias[l], ssd_a_log[l], ctx_len, cl)

        if ctx_out:
            a_all = jnp.concatenate([a_lat, a_ctx], 0)
            m_all = jnp.concatenate([m_lat, m_ctx], 0)
            n_tiles = n
        else:
            a_all, m_all = a_lat, m_lat
            n_tiles = nlt
        x1 = _merge(x_lat, x_ctx, ctx_tile0, mod[l], nlt, n_tiles, h, a_all, hf, hb, mlo, ml_norm[l][None],
                    m_all, yf, yb, xs, z, jnp.repeat(ssd_d[l], SSD_P)[None], ssd_norm[l][None],
                    w_gate[l].astype(BF16), b_gate[l], w_branch[l].astype(BF16), w_o[l].astype(BF16),
                    ln1_g[l][None], ln1_b[l][None], tm, alpha)
        x_lat = _mlp(x1, mod[l], nlt, w_up[l].astype(BF16), b_up[l][None],
                     w_down[l].astype(BF16), b_down[l][None], ln2_g[l][None], ln2_b[l][None], tm, alpha)
        x_ctx, ctx_tile0 = x_lat, nlt
    return x_lat[None]
```

```python
import functools
import math

import jax
import jax.numpy as jnp
from jax import lax
from jax.experimental import pallas as pl
from jax.experimental.pallas import tpu as pltpu

F32 = jnp.float32
BF16 = jnp.bfloat16

GRID_W = 64
ROPE_BASE = 10000.0
CONV_K = 5
EPS = 1e-6
LOG2E = math.log2(math.e)
LANES = 128
CONV_HALO = 8
SCAN_CHUNK = 128

DIFF_H, DIFF_HD = 4, 64
ML_H, ML_DK, ML_DV = 4, 64, 128
MLA_H, MLA_Q_RANK, MLA_KV_RANK, MLA_NOPE, MLA_ROPE, MLA_V = 4, 384, 256, 64, 32, 128
SSD_H, SSD_P, SSD_G, SSD_N = 8, 64, 2, 64
D_INNER = SSD_H * SSD_P

C_QD, C_KD, C_VD, C_MLQK, C_MLV, C_MLO, C_Z, C_XBC, C_CQ, C_CKV, C_SMALL, C_END = (
    0, 512, 1024, 1536, 2048, 2560, 3072, 3584, 4352, 4736, 4992, 5120)
SM_GATE, SM_DT, SM_KR = 0, 16, 64

VMEM_LIMIT = 60 * 1024 * 1024


def _cparams(sem):
    return pltpu.CompilerParams(dimension_semantics=sem, vmem_limit_bytes=VMEM_LIMIT)


def _ln(x):
    mu = jnp.mean(x, -1, keepdims=True)
    xc = x - mu
    var = jnp.mean(xc * xc, -1, keepdims=True)
    return xc * lax.rsqrt(var + EPS)


def _sigmoid(x):
    return 0.5 * jnp.tanh(0.5 * x) + 0.5


def _softplus(x):
    return jnp.maximum(x, 0.0) + jnp.log1p(jnp.exp(-jnp.abs(x)))


def _bdot(a, b):
    return jnp.dot(a.astype(BF16), b.astype(BF16), preferred_element_type=F32)


def _hdot(a, b):
    return jnp.dot(a, b, precision=lax.Precision.HIGHEST, preferred_element_type=F32)


def _select_dot(x, sel, x_is_lhs):
    sel = sel.astype(BF16)
    acc, rest = None, x
    for _ in range(3):
        part = rest.astype(BF16)
        prod = (jnp.dot(part, sel, preferred_element_type=F32) if x_is_lhs
                else jnp.dot(sel, part, preferred_element_type=F32))
        acc = prod if acc is None else acc + prod
        rest = rest - part.astype(F32)
    return acc


def _mod_kernel(c_ref, w_ref, b_ref, o_ref):
    c = c_ref[...]
    o_ref[...] = _hdot(c * _sigmoid(c), w_ref[...]) + b_ref[...]


def _modulation(cvec, w_mod, b_mod):
    depth, d, n = w_mod.shape
    tn = 1536
    return pl.pallas_call(
        _mod_kernel,
        grid=(depth, n // tn),
        in_specs=[pl.BlockSpec((8, d), lambda l, j: (0, 0)),
                  pl.BlockSpec((None, d, tn), lambda l, j: (l, 0, j)),
                  pl.BlockSpec((None, 1, tn), lambda l, j: (l, 0, j))],
        out_specs=pl.BlockSpec((None, 8, tn), lambda l, j: (l, 0, j)),
        out_shape=jax.ShapeDtypeStruct((depth, 8, n), F32),
        compiler_params=_cparams(("arbitrary", "arbitrary")),
        name="modulation",
    )(cvec, w_mod, b_mod.reshape(depth, 1, n))


def _rope(x, cos, sin_next, sin_prev, partner):
    return (x * cos + pltpu.roll(x, LANES - partner, 1) * sin_next
            + pltpu.roll(x, partner, 1) * sin_prev)


def _rms(x, w):
    return x * lax.rsqrt(jnp.mean(x * x, -1, keepdims=True) + EPS) * w


def _inproj_kernel(xl_ref, xc_ref, mod_ref, win_ref, roper_ref, ropec_ref, qn_ref, kvn_ref, wuq_ref, wuk_ref, wuv_ref,
                   h_ref, qd_ref, kd_ref, vd_ref, mlqk_ref, mlv_ref, mlo_ref, z_ref, xbc_ref,
                   small_ref, qm_ref, km_ref, vm_ref, *, nlt):
    x = _stream_tile(xl_ref, xc_ref, nlt)
    h = _ln(x) * (1.0 + mod_ref[1:2, :]) + mod_ref[0:1, :]
    hb = h.astype(BF16)
    h_ref[...] = hb

    def proj(lo, hi):
        return jnp.dot(hb, win_ref[:, lo:hi], preferred_element_type=F32)

    col_part = jnp.where(pl.program_id(0) < nlt, ropec_ref[...], 0.0)
    rope = jnp.concatenate([roper_ref[g] + col_part for g in range(x.shape[0] // GRID_W)], 0)
    cos_d, sn_d, sp_d = rope[:, 0:128], rope[:, 128:256], rope[:, 256:384]
    cos_m, sn_m, sp_m = rope[:, 384:512], rope[:, 512:640], rope[:, 640:768]

    lane = lax.broadcasted_iota(jnp.int32, (x.shape[0], LANES), 1)
    one_col = (lane == 0).astype(BF16)

    qd = proj(C_QD, C_KD)
    kd = proj(C_KD, C_VD)
    vd = proj(C_VD, C_MLQK).astype(BF16)
    for hh in range(DIFF_H):
        sl = slice(hh * LANES, (hh + 1) * LANES)
        qd_ref[:, sl] = (_rope(qd[:, sl], cos_d, sn_d, sp_d, 16) * (DIFF_HD ** -0.5 * LOG2E)).astype(BF16)
        kd_ref[:, sl] = _rope(kd[:, sl], cos_d, sn_d, sp_d, 16).astype(BF16)
        vd_ref[:, 2 * hh * LANES:(2 * hh + 1) * LANES] = vd[:, sl]
        vd_ref[:, (2 * hh + 1) * LANES:(2 * hh + 2) * LANES] = one_col
    mlqk_ref[...] = proj(C_MLQK, C_MLV)
    mlv_ref[...] = proj(C_MLV, C_MLO).astype(BF16)
    mlo_ref[...] = proj(C_MLO, C_Z).astype(BF16)
    z_ref[...] = proj(C_Z, C_XBC).astype(BF16)
    xbc_ref[...] = proj(C_XBC, C_CQ)
    small = proj(C_SMALL, C_END)
    small_ref[...] = small

    cq = _rms(proj(C_CQ, C_CKV), qn_ref[...])
    ckv = _rms(proj(C_CKV, C_SMALL), kvn_ref[...])
    qm = _bdot(cq, wuq_ref[...])
    km = _bdot(ckv, wuk_ref[...])
    vm = _bdot(ckv, wuv_ref[...]).astype(BF16)
    kr = jnp.where((lane >= SM_KR) & (lane < SM_KR + MLA_ROPE), small, 0.0)
    kr = _rope(kr, cos_m, sn_m, sp_m, 8)
    scale = (MLA_NOPE + MLA_ROPE) ** -0.5 * LOG2E
    for hh in range(MLA_H):
        sl = slice(hh * LANES, (hh + 1) * LANES)
        qm_ref[:, sl] = (_rope(qm[:, sl], cos_m, sn_m, sp_m, 8) * scale).astype(BF16)
        km_ref[:, sl] = (km[:, sl] + kr).astype(BF16)
        vm_ref[:, 2 * hh * LANES:(2 * hh + 1) * LANES] = vm[:, sl]
        vm_ref[:, (2 * hh + 1) * LANES:(2 * hh + 2) * LANES] = one_col


def _stream_specs(tm, d, nlt, ctx_tile0):
    return [pl.BlockSpec((tm, d), lambda i: (jnp.minimum(i, nlt - 1), 0)),
            pl.BlockSpec((tm, d), lambda i: (jnp.maximum(i - nlt, 0) + ctx_tile0, 0))]


def _stream_tile(xl_ref, xc_ref, nlt):
    return jnp.where(pl.program_id(0) < nlt, xl_ref[...], xc_ref[...])


def _inproj(x_lat, x_ctx, ctx_tile0, s, mod, nlt, w_in, rope, qn, kvn, wuq, wuk, wuv, tm):
    d = x_lat.shape[1]
    n = s // tm
    rope_rows, rope_cols = rope
    row = lambda w: pl.BlockSpec((tm, w), lambda i: (i, 0))
    full = lambda a: pl.BlockSpec(a.shape, lambda i: (0,) * a.ndim)
    outs = [(d, BF16), (512, BF16), (512, BF16), (1024, BF16), (512, F32), (512, BF16), (512, BF16),
            (512, BF16), (768, F32), (128, F32), (512, BF16), (512, BF16), (1024, BF16)]
    return pl.pallas_call(
        functools.partial(_inproj_kernel, nlt=nlt),
        grid=(n,),
        in_specs=_stream_specs(tm, d, nlt, ctx_tile0) + [
            pl.BlockSpec((None, 8, d), lambda i: (jnp.where(i < nlt, 0, 1), 0, 0)),
            full(w_in), pl.BlockSpec((tm // GRID_W, 1, rope_rows.shape[2]), lambda i: (i, 0, 0)),
            full(rope_cols), full(qn), full(kvn), full(wuq), full(wuk), full(wuv)],
        out_specs=[row(w) for w, _ in outs],
        out_shape=[jax.ShapeDtypeStruct((s, w), dt) for w, dt in outs],
        compiler_params=_cparams(("arbitrary",)),
        name="inproj",
    )(x_lat, x_ctx, mod, w_in, rope_rows, rope_cols, qn, kvn, wuq, wuk, wuv)


def _conv_kernel(cur_ref, prev_ref, next_ref, w_ref, b_ref, *rest, splits, scales, nlt, n):
    out_refs, ext_ref = rest[:-1], rest[-1]
    i = pl.program_id(0)
    tm = cur_ref.shape[0]
    has_prev = jnp.logical_and(i != 0, i != nlt).astype(F32)
    has_next = jnp.logical_and(i != nlt - 1, i != n - 1).astype(F32)
    ext_ref[0:CONV_HALO, :] = prev_ref[...] * has_prev
    ext_ref[CONV_HALO:CONV_HALO + tm, :] = cur_ref[...]
    ext_ref[CONV_HALO + tm:, :] = next_ref[...] * has_next
    acc = b_ref[...] + w_ref[0:1, :] * ext_ref[pl.ds(CONV_HALO - 2, tm), :]
    for k in range(1, CONV_K):
        acc = acc + w_ref[k:k + 1, :] * ext_ref[pl.ds(CONV_HALO - 2 + k, tm), :]
    y = acc * _sigmoid(acc)
    lo = 0
    for ref, wd, sc in zip(out_refs, splits, scales):
        ref[...] = (y[:, lo:lo + wd] * sc).astype(ref.dtype)
        lo += wd


def _conv_silu(xin, w, b, nlt, tm, splits, scales):
    s, c = xin.shape
    n = s // tm
    r = tm // CONV_HALO
    wpad = jnp.zeros((8, c), F32).at[:CONV_K].set(w)
    return pl.pallas_call(
        functools.partial(_conv_kernel, splits=splits, scales=scales, nlt=nlt, n=n),
        grid=(n,),
        in_specs=[pl.BlockSpec((tm, c), lambda i: (i, 0)),
                  pl.BlockSpec((CONV_HALO, c), lambda i: (jnp.maximum(i * r - 1, 0), 0)),
                  pl.BlockSpec((CONV_HALO, c), lambda i: (jnp.minimum((i + 1) * r, n * r - 1), 0)),
                  pl.BlockSpec((8, c), lambda i: (0, 0)),
                  pl.BlockSpec((1, c), lambda i: (0, 0))],
        out_specs=[pl.BlockSpec((tm, wd), lambda i: (i, 0)) for wd in splits],
        out_shape=[jax.ShapeDtypeStruct((s, wd), BF16) for wd in splits],
        scratch_shapes=[pltpu.VMEM((tm + 2 * CONV_HALO, c), F32)],
        compiler_params=_cparams(("arbitrary",)),
        name="conv_silu",
    )(xin, xin, xin, wpad, b.reshape(1, c))


def _flash_kernel(q_ref, k_ref, v_ref, lam_ref, sub_ref, o_ref, qs_ref, s_buf, p_buf, a_buf, mx_buf, m_ref, acc_ref,
                  *, diff, tk, nkv, lam_init):
    tq = q_ref.shape[0]
    if diff:
        q = q_ref[...]
        lane = lax.broadcasted_iota(jnp.int32, q.shape, 1)
        zero = jnp.zeros_like(q)
        qs_ref[0:tq, :] = jnp.where(lane < DIFF_HD, q, zero)
        qs_ref[tq:, :] = jnp.where(lane >= DIFF_HD, q, zero)
    else:
        qs_ref[...] = q_ref[...]
    m_ref[...] = jnp.full(m_ref.shape, -jnp.inf, F32)
    acc_ref[...] = jnp.zeros_like(acc_ref)

    def scores(j, slot):
        off = pl.multiple_of(j * tk, tk)
        s_buf[slot] = lax.dot_general(qs_ref[...], k_ref[pl.ds(off, tk), :], (((1,), (1,)), ((), ())),
                                      preferred_element_type=F32)

    def softmax(slot):
        s = s_buf[slot]
        m_old = m_ref[...]
        m_new = jnp.maximum(m_old, jnp.max(s, -1, keepdims=True))
        p_buf[slot] = jnp.exp2(s - m_new).astype(BF16)
        a_buf[slot] = jnp.exp2(m_old - m_new)
        m_ref[...] = m_new

    def values(j, slot):
        off = pl.multiple_of(j * tk, tk)
        acc_ref[...] = a_buf[slot] * acc_ref[...] + jnp.dot(p_buf[slot], v_ref[pl.ds(off, tk), :],
                                                            preferred_element_type=F32)

    scores(0, 0)
    if nkv > 1:
        scores(1, 1)
    softmax(0)

    def step(j, slot):
        scores(j + 1, 1 - slot)
        softmax(slot)
        values(j - 1, 1 - slot)

    def body(jj, carry):
        step(2 * jj + 1, 1)
        step(2 * jj + 2, 0)
        return carry

    pairs = max(nkv - 2, 0) // 2
    if pairs:
        lax.fori_loop(0, pairs, body, 0, unroll=True)
    if max(nkv - 2, 0) % 2:
        step(nkv - 2, (nkv - 2) % 2)
    if nkv > 1:
        softmax((nkv - 1) % 2)
        values(nkv - 2, nkv % 2)
    values(nkv - 1, (nkv - 1) % 2)

    acc = acc_ref[...]
    o = acc[:, :LANES] / acc[:, LANES:LANES + 1]
    if diff:
        lam = lam_ref[...]
        lam_full = (jnp.exp(jnp.sum(lam[0:1, :] * lam[1:2, :], -1, keepdims=True))
                    - jnp.exp(jnp.sum(lam[2:3, :] * lam[3:4, :], -1, keepdims=True)) + lam_init)
        o = o[:tq] - lam_full * o[tq:]
        o = _rms(o, sub_ref[...]) * (1.0 - lam_init)
    o_ref[...] = o.astype(o_ref.dtype)


def _flash(q, k, v, lam, subln, *, diff, q_tile0, nq, kv_block, kv_len, tq, tk, lam_init=0.0):
    heads = q.shape[1] // LANES
    rows = 2 * tq if diff else tq
    return pl.pallas_call(
        functools.partial(_flash_kernel, diff=diff, tk=tk, nkv=kv_len // tk, lam_init=lam_init),
        grid=(heads, nq),
        in_specs=[pl.BlockSpec((tq, LANES), lambda h, i: (i + q_tile0, h)),
                  pl.BlockSpec((kv_len, LANES), lambda h, i: (kv_block, h)),
                  pl.BlockSpec((kv_len, 2 * LANES), lambda h, i: (kv_block, h)),
                  pl.BlockSpec(lam.shape, lambda h, i: (0, 0)),
                  pl.BlockSpec(subln.shape, lambda h, i: (0, 0))],
        out_specs=pl.BlockSpec((tq, LANES), lambda h, i: (i, h)),
        out_shape=jax.ShapeDtypeStruct((nq * tq, heads * LANES), BF16),
        scratch_shapes=[pltpu.VMEM((rows, LANES), BF16),
                        pltpu.VMEM((2, rows, tk), F32),
                        pltpu.VMEM((2, rows, tk), BF16),
                        pltpu.VMEM((2, rows, 1), F32),
                        pltpu.VMEM((2, rows, 1), F32),
                        pltpu.VMEM((rows, 1), F32),
                        pltpu.VMEM((rows, 2 * LANES), F32)],
        compiler_params=_cparams(("arbitrary", "arbitrary")),
        name="flash_diff" if diff else "flash_mla",
    )(q, k, v, lam, subln)


def _attention(q, k, v, lam, subln, *, diff, ctx_len, ctx_out, tq, tk, lam_init=0.0):
    s = q.shape[0]
    t = s - ctx_len
    lat = _flash(q, k, v, lam, subln, diff=diff, q_tile0=0, nq=t // tq, kv_block=0, kv_len=s,
                 tq=tq, tk=tk, lam_init=lam_init)
    if not ctx_out:
        return lat, None
    tqc = math.gcd(tq, ctx_len)
    ctx = _flash(q, k, v, lam, subln, diff=diff, q_tile0=t // tqc, nq=ctx_len // tqc, kv_block=t // ctx_len,
                 kv_len=ctx_len, tq=tqc, tk=math.gcd(tk, ctx_len), lam_init=lam_init)
    return lat, ctx


def _tri(n, upper):
    r = lax.broadcasted_iota(jnp.int32, (n, n), 0)
    c = lax.broadcasted_iota(jnp.int32, (n, n), 1)
    return (c >= r) if upper else (c <= r)


def _mlstm_dir(bwd, q_ref, kt_ref, v_ref, sm_ref, smt_ref, gb_ref, gbt_ref, o_ref, c_ref, m_ref):
    cl = q_ref.shape[0]
    assert cl == LANES == ML_DV
    allowed = _tri(cl, bwd)
    pre = sm_ref[...] + gb_ref[...]
    pre_t = smt_ref[...] + gbt_ref[...]
    lf = -_softplus(-pre)
    lf_t = -_softplus(-pre_t)
    g_col = _select_dot(lf, allowed, False)
    g_row = _select_dot(lf_t, _tri(cl, not bwd), True)
    q = q_ref[...]
    kt = kt_ref[...]
    lane_q = lax.broadcasted_iota(jnp.int32, q.shape, 1)
    ones = jnp.ones((cl, LANES), BF16)
    end = 0 if bwd else cl - 1
    twice = lambda a: jnp.concatenate([a, a], 1)
    for hh in range(ML_H):
        ci, cf = (2 * bwd) * ML_H + hh, (2 * bwd + 1) * ML_H + hh
        g_c = jnp.broadcast_to(g_col[:, cf:cf + 1], (cl, LANES))
        i_c = jnp.broadcast_to(pre[:, ci:ci + 1], (cl, LANES))
        g_r, i_r = g_row[cf:cf + 1, :], pre_t[ci:ci + 1, :]
        m_prev = m_ref[hh:hh + 1, :]
        qh = jnp.where((lane_q >= hh * ML_DK) & (lane_q < (hh + 1) * ML_DK), q, jnp.zeros_like(q))
        vh = jnp.concatenate([v_ref[:, hh * ML_DV:(hh + 1) * ML_DV], ones], 1)
        logw = jnp.where(allowed, g_c - g_r + i_r, -jnp.inf)
        log_inter = g_c + m_prev
        m_loc = jnp.broadcast_to(jnp.max(logw, -1, keepdims=True), (cl, LANES))
        m_j = jnp.maximum(log_inter, m_loc)
        w = jnp.exp(logw - m_j) * jnp.dot(qh, kt, preferred_element_type=F32)
        a_inter = jnp.exp(log_inter - m_j)
        cstate = c_ref[...]
        nd = _bdot(w, vh) + twice(a_inter) * _bdot(qh, cstate)
        den = jnp.broadcast_to(nd[:, ML_DV:ML_DV + 1], (cl, LANES))
        o_ref[:, hh * ML_DV:(hh + 1) * ML_DV] = nd[:, :ML_DV] / jnp.maximum(jnp.abs(den), jnp.exp(-m_j))
        g_end = g_c[end:end + 1, :]
        log_s = g_end - g_c + i_c
        m_new = jnp.maximum(g_end + m_prev, jnp.max(log_s, 0, keepdims=True))
        ws = jnp.exp(log_s - m_new)
        decay = jnp.exp(g_end + m_prev - m_new)
        rows = slice(hh * ML_DK, (hh + 1) * ML_DK)
        upd = jnp.dot(kt[rows, :], (twice(ws) * vh.astype(F32)).astype(BF16), preferred_element_type=F32)
        c_ref[rows, :] = twice(decay) * cstate[rows, :] + upd
        m_ref[hh:hh + 1, :] = m_new


def _ssd_dir(bwd, xs_ref, bt_ref, cm_ref, sm_ref, smt_ref, db_ref, dbt_ref, al_ref, alt_ref, ex_ref,
             o_ref, st_ref):
    cl = xs_ref.shape[0]
    allowed = _tri(cl, bwd)
    dt = _softplus(sm_ref[...] + db_ref[...])
    dt_t = _softplus(smt_ref[...] + dbt_ref[...])
    a = dt * (-jnp.exp(al_ref[...]))
    a_t = dt_t * (-jnp.exp(alt_ref[...]))
    s_col = _select_dot(a, allowed, False)
    s_row = _select_dot(a_t, _tri(cl, not bwd), True)
    end = 0 if bwd else cl - 1
    s_end = s_col[end:end + 1, :]
    stack = jnp.concatenate([jnp.exp(s_col), dt, dt * jnp.exp(s_end - s_col), jnp.exp(jnp.broadcast_to(s_end, (8, LANES)))], 0)
    wide = _select_dot(stack, ex_ref[bwd], True)
    es_w, dt_w, wt_w, dec_w = wide[:cl], wide[cl:2 * cl], wide[2 * cl:3 * cl], wide[3 * cl:3 * cl + 1]
    xs = xs_ref[...].astype(F32)
    xdt = (xs * dt_w).astype(BF16)
    cm = cm_ref[...]
    bt = bt_ref[...]
    lane_c = lax.broadcasted_iota(jnp.int32, cm.shape, 1)
    lane_x = lax.broadcasted_iota(jnp.int32, (cl, LANES), 1)
    state = st_ref[...]
    y = es_w * _bdot(cm, state)
    per_pair = SSD_H // (D_INNER // LANES)
    cbs = []
    for grp in range(SSD_G):
        cg = jnp.where((lane_c >= grp * SSD_N) & (lane_c < (grp + 1) * SSD_N), cm, jnp.zeros_like(cm))
        cbs.append(jnp.dot(cg, bt, preferred_element_type=F32))
    for blk in range(D_INNER // LANES):
        cb = cbs[(blk * per_pair) // (SSD_H // SSD_G)]
        xblk = xdt[:, blk * LANES:(blk + 1) * LANES]
        res = []
        for sub in range(per_pair):
            hh = blk * per_pair + sub
            col = SM_DT + bwd * SSD_H + hh
            decay = jnp.exp(jnp.where(allowed, s_col[:, col:col + 1] - s_row[col:col + 1, :], -jnp.inf))
            res.append(_bdot(decay * cb, xblk))
        intra = jnp.where(lane_x < SSD_P, res[0], res[1])
        o_ref[:, blk * LANES:(blk + 1) * LANES] = y[:, blk * LANES:(blk + 1) * LANES] + intra
    upd = jnp.dot(bt, (xs * wt_w).astype(BF16), preferred_element_type=F32)
    r = lax.broadcasted_iota(jnp.int32, upd.shape, 0) // SSD_N
    c = lax.broadcasted_iota(jnp.int32, upd.shape, 1) // (SSD_P * (SSD_H // SSD_G))
    st_ref[...] = jnp.where(r == c, dec_w * state + upd, 0.0)


def _scan_kernel(qf, ktf, vf, xf, btf, cmf, smf, smtf, qb, ktb, vb, xb, btb, cmb, smb, smtb,
                 gb_ref, gbt_ref, db_ref, dbt_ref, al_ref, alt_ref, ex_ref,
                 hf_ref, hb_ref, yf_ref, yb_ref, cf_ref, mf_ref, cb_ref, mb_ref, sf_ref, sb_ref):
    @pl.when(pl.program_id(0) == 0)
    def _():
        for ref in (cf_ref, mf_ref, cb_ref, mb_ref, sf_ref, sb_ref):
            ref[...] = jnp.zeros_like(ref)

    cl = SCAN_CHUNK
    per_step = qf.shape[0] // cl
    rows = lambda ref, k: ref.at[pl.ds(k * cl, cl), :]
    cols = lambda ref, k: ref.at[:, pl.ds(k * cl, cl)]
    for sub in range(per_step):
        f, b = sub, per_step - 1 - sub
        _mlstm_dir(0, rows(qf, f), cols(ktf, f), rows(vf, f), rows(smf, f), cols(smtf, f), gb_ref, gbt_ref,
                   rows(hf_ref, f), cf_ref, mf_ref)
        _ssd_dir(0, rows(xf, f), cols(btf, f), rows(cmf, f), rows(smf, f), cols(smtf, f), db_ref, dbt_ref,
                 al_ref, alt_ref, ex_ref, rows(yf_ref, f), sf_ref)
        _mlstm_dir(1, rows(qb, b), cols(ktb, b), rows(vb, b), rows(smb, b), cols(smtb, b), gb_ref, gbt_ref,
                   rows(hb_ref, b), cb_ref, mb_ref)
        _ssd_dir(1, rows(xb, b), cols(btb, b), rows(cmb, b), rows(smb, b), cols(smtb, b), db_ref, dbt_ref,
                 al_ref, alt_ref, ex_ref, rows(yb_ref, b), sb_ref)


def _scan_orders(nc, ncc):
    nlc = nc - ncc
    fwd = lambda c: jnp.where(c < ncc, nlc + c, c - ncc)
    bwd = lambda c: nc - 1 - c
    return fwd, bwd


def _scans(q, kt, v, xs, bt, cm, small, small_t, gate_b, dt_bias, a_log, ctx_len, cl):
    s = xs.shape[0]
    nc, ncc = s // cl, ctx_len // cl
    gb = jnp.zeros((1, LANES), F32).at[0, SM_GATE:SM_GATE + 4 * ML_H].set(gate_b.reshape(-1))
    db = jnp.zeros((1, LANES), F32).at[0, SM_DT:SM_DT + 2 * SSD_H].set(dt_bias.reshape(-1))
    al = jnp.zeros((1, LANES), F32).at[0, SM_DT:SM_DT + 2 * SSD_H].set(a_log.reshape(-1))
    src = jnp.arange(LANES)[None, :, None]
    dst_head = (jnp.arange(D_INNER) // SSD_P)[None, None, :]
    dirs = jnp.arange(2)[:, None, None]
    expand = (src == SM_DT + dirs * SSD_H + dst_head).astype(BF16)

    def specs(order):
        rows = lambda w: pl.BlockSpec((cl, w), lambda c: (order(c), 0))
        cols = lambda h: pl.BlockSpec((h, cl), lambda c: (0, order(c)))
        return [rows(ML_H * ML_DK), cols(ML_H * ML_DK), rows(ML_H * ML_DV),
                rows(D_INNER), cols(SSD_G * SSD_N), rows(SSD_G * SSD_N), rows(LANES), cols(LANES)]

    fwd, bwd = _scan_orders(nc, ncc)
    const = lambda a: pl.BlockSpec(a.shape, lambda c: (0,) * a.ndim)
    gbt, dbt, alt = gb.reshape(LANES, 1), db.reshape(LANES, 1), al.reshape(LANES, 1)
    consts = [gb, gbt, db, dbt, al, alt, expand]
    ml_state = [pltpu.VMEM((ML_H * ML_DK, 2 * ML_DV), F32), pltpu.VMEM((8, LANES), F32)]
    ssd_state = pltpu.VMEM((SSD_G * SSD_N, D_INNER), F32)
    out = lambda order, w: pl.BlockSpec((cl, w), lambda c: (order(c), 0))
    operands = (q, kt, v, xs, bt, cm, small, small_t)
    return pl.pallas_call(
        _scan_kernel,
        grid=(nc,),
        in_specs=specs(fwd) + specs(bwd) + [const(a) for a in consts],
        out_specs=[out(fwd, ML_H * ML_DV), out(bwd, ML_H * ML_DV), out(fwd, D_INNER), out(bwd, D_INNER)],
        out_shape=[jax.ShapeDtypeStruct((s, ML_H * ML_DV), F32)] * 2
        + [jax.ShapeDtypeStruct((s, D_INNER), F32)] * 2,
        scratch_shapes=ml_state + ml_state + [ssd_state, ssd_state],
        compiler_params=_cparams(("arbitrary",)),
        name="scans",
    )(*operands, *operands, *consts)


def _merge_kernel(xl_ref, xc_ref, mod_ref, h_ref, a_ref, hf_ref, hb_ref, mlo_ref, mln_ref, m_ref, yf_ref,
                  yb_ref, xs_ref, z_ref, dsk_ref, sdn_ref, wg_ref, bg_ref, wb_ref, wo_ref, g_ref, b_ref, o_ref,
                  *, alpha, nlt):
    hb16 = h_ref[...]
    hm = hf_ref[...] + hb_ref[...]
    og = _sigmoid(mlo_ref[...].astype(F32))
    b_parts = []
    for hh in range(ML_H):
        sl = slice(hh * ML_DV, (hh + 1) * ML_DV)
        b_parts.append(_rms(hm[:, sl], mln_ref[:, sl]) * og[:, sl])
    b_br = jnp.concatenate(b_parts, 1)
    z = z_ref[...].astype(F32)
    ys = (yf_ref[...] + yb_ref[...] + dsk_ref[...] * xs_ref[...].astype(F32)) * (z * _sigmoid(z))
    gw = D_INNER // SSD_G
    s_br = jnp.concatenate([_rms(ys[:, g * gw:(g + 1) * gw], sdn_ref[:, g * gw:(g + 1) * gw])
                            for g in range(SSD_G)], 1)
    branches = (a_ref[...], b_br, m_ref[...], s_br)
    y = None
    for k, br in enumerate(branches):
        gate = _sigmoid(jnp.dot(hb16, wg_ref[k], preferred_element_type=F32) + bg_ref[k:k + 1, :])
        term = gate * _bdot(br, wb_ref[k])
        y = term if y is None else y + term
    y = _bdot(y, wo_ref[...])
    x1 = alpha * _stream_tile(xl_ref, xc_ref, nlt) + mod_ref[2:3, :] * y
    o_ref[...] = _ln(x1) * g_ref[...] + b_ref[...]


def _merge(x_lat, x_ctx, ctx_tile0, mod, nlt, n_tiles, h, a, hf, hb, mlo, mln, m, yf, yb, xs, z, dsk, sdn, wg,
           bg, wb, wo, g, b, tm, alpha):
    d = x_lat.shape[1]
    row = lambda w: pl.BlockSpec((tm, w), lambda i: (i, 0))
    full = lambda arr: pl.BlockSpec(arr.shape, lambda i: (0,) * arr.ndim)
    return pl.pallas_call(
        functools.partial(_merge_kernel, alpha=alpha, nlt=nlt),
        grid=(n_tiles,),
        in_specs=_stream_specs(tm, d, nlt, ctx_tile0) + [
            pl.BlockSpec((None, 8, d), lambda i: (jnp.where(i < nlt, 0, 1), 0, 0)),
            row(d), row(512), row(512), row(512), row(512), full(mln), row(512), row(512), row(512),
            row(512), row(512), full(dsk), full(sdn), full(wg), full(bg), full(wb), full(wo),
            full(g), full(b)],
        out_specs=row(d),
        out_shape=jax.ShapeDtypeStruct((n_tiles * tm, d), F32),
        compiler_params=_cparams(("arbitrary",)),
        name="merge_ln1",
    )(x_lat, x_ctx, mod, h, a, hf, hb, mlo, mln, m, yf, yb, xs, z, dsk, sdn, wg, bg, wb, wo, g, b)


def _mlp_kernel(x_ref, mod_ref, wu_ref, bu_ref, wd_ref, bd_ref, g_ref, b_ref, o_ref, *, alpha):
    x = x_ref[...]
    hm = _ln(x) * (1.0 + mod_ref[4:5, :]) + mod_ref[3:4, :]
    u = jnp.maximum(_bdot(hm, wu_ref[...]) + bu_ref[...], 0.0)
    f = _bdot(u * u, wd_ref[...]) + bd_ref[...]
    x2 = alpha * x + mod_ref[5:6, :] * f
    o_ref[...] = _ln(x2) * g_ref[...] + b_ref[...]


def _mlp(x1, mod, nlt, wu, bu, wd, bd, g, b, tm, alpha):
    s, d = x1.shape
    n_tiles = s // tm
    full = lambda arr: pl.BlockSpec(arr.shape, lambda i: (0,) * arr.ndim)
    return pl.pallas_call(
        functools.partial(_mlp_kernel, alpha=alpha),
        grid=(n_tiles,),
        in_specs=[pl.BlockSpec((tm, d), lambda i: (i, 0)),
                  pl.BlockSpec((None, 8, d), lambda i: (jnp.where(i < nlt, 0, 1), 0, 0)),
                  full(wu), full(bu), full(wd), full(bd), full(g), full(b)],
        out_specs=pl.BlockSpec((tm, d), lambda i: (i, 0)),
        out_shape=jax.ShapeDtypeStruct((n_tiles * tm, d), F32),
        compiler_params=_cparams(("arbitrary",)),
        name="mlp_ln2",
    )(x1, mod, wu, bu, wd, bd, g, b)


def _axis_tables(pos, dsub):
    half = dsub // 2
    inv = ROPE_BASE ** (-jnp.arange(half, dtype=F32) * 2.0 / dsub)
    ang = pos[:, None] * inv[None, :]
    c, s = jnp.cos(ang), jnp.sin(ang)
    z = jnp.zeros_like(s)
    return jnp.concatenate([c, c], -1), jnp.concatenate([-s, z], -1), jnp.concatenate([z, s], -1)


def _rope_tables(t, ctx_len):
    n_rows = t // GRID_W
    pad = LANES - MLA_NOPE - MLA_ROPE

    def parts(pos, is_row):
        n = pos.shape[0]
        zeros = lambda w: jnp.zeros((n, w), F32)
        fill = lambda w, cos: jnp.ones((n, w), F32) if (cos and is_row) else zeros(w)
        out = []
        for k, a in enumerate(_axis_tables(pos, DIFF_HD // 2)):
            one_map = jnp.concatenate([a, zeros(DIFF_HD // 2)] if is_row else [zeros(DIFF_HD // 2), a], -1)
            out.append(jnp.tile(one_map, (1, 2)))
        for k, a in enumerate(_axis_tables(pos, MLA_ROPE // 2)):
            rot = jnp.concatenate([a, zeros(MLA_ROPE // 2)] if is_row else [zeros(MLA_ROPE // 2), a], -1)
            out.append(jnp.concatenate([fill(MLA_NOPE, k == 0), rot, fill(pad, k == 0)], -1))
        return jnp.concatenate(out, -1)

    rows = parts(jnp.arange(n_rows, dtype=F32), True)
    cols = parts(jnp.arange(GRID_W, dtype=F32), False)
    n_ctx = ctx_len // GRID_W
    ident = jnp.concatenate([jnp.ones((n_ctx, LANES), F32), jnp.zeros((n_ctx, 2 * LANES), F32)] * 2, -1)
    return jnp.concatenate([rows, ident], 0)[:, None, :], cols


def _relayout_w_in(w):
    d = w.shape[0]
    o = [0, 512, 1024, 1536, 2048, 2560, 3072, 3088, 3472, 3728, 3760, 4272, 5040, 5056]
    seg = lambda i: w[:, o[i]:o[i + 1]]
    qd, kd, vd, mlqk, mlv, mlo, mlg, cq, ckv, kr, z, xbc, dt = [seg(i) for i in range(13)]
    zc = lambda n: jnp.zeros((d, n), w.dtype)
    small = jnp.concatenate([mlg, dt, zc(SM_KR - 32), kr, zc(LANES - SM_KR - MLA_ROPE)], 1)
    return jnp.concatenate([qd, kd, vd, mlqk, mlv, mlo, z, xbc, cq, ckv, small], 1).astype(BF16)


def _pad_heads(w, heads, width):
    r = w.shape[0]
    w = w.reshape(r, heads, width)
    return jnp.pad(w, ((0, 0), (0, 0), (0, LANES - width))).reshape(r, heads * LANES)


def _pick_tile(n, pref):
    for t in pref:
        if n % t == 0:
            return t
    raise ValueError(f"no tile for {n}")


def kernel(x, c, ctx, c_ctx, w_mod, b_mod, w_in, diff_lambda, diff_subln, ml_conv_w, ml_conv_b, ml_gate_b, ml_norm, mla_q_norm, mla_kv_norm, mla_w_uq, mla_w_ukv, ssd_conv_w, ssd_conv_b, ssd_dt_bias, ssd_a_log, ssd_d, ssd_norm, w_gate, b_gate, w_branch, w_o, ln1_g, ln1_b, w_up, b_up, w_down, b_down, ln2_g, ln2_b):
    assert x.shape[0] == 1 and ctx.shape[0] == 1
    depth = w_in.shape[0]
    t, d = x.shape[1], x.shape[2]
    ctx_len = ctx.shape[1]
    s = ctx_len + t
    tm = _pick_tile(math.gcd(ctx_len, t), (256, 128))
    cl = _pick_tile(math.gcd(ctx_len, t), (2 * SCAN_CHUNK, SCAN_CHUNK))
    tq_diff = _pick_tile(t, (256, 128))
    tq_mla = _pick_tile(t, (512, 256, 128))
    tk = _pick_tile(s, (3328, 1280, 640, 256, 128))
    assert ctx_len % cl == 0 and t % cl == 0 and t % GRID_W == 0 and t % ctx_len == 0
    nlt, n = t // tm, s // tm
    alpha = (2 * depth) ** 0.25

    x_lat, x_ctx, ctx_tile0 = x[0], ctx[0], 0
    cvec = jnp.zeros((8, d), F32).at[0].set(c[0]).at[1].set(c_ctx)
    mod = _modulation(cvec, w_mod, b_mod)
    mod = jnp.pad(mod[:, :2].reshape(depth, 2, 6, d), ((0, 0), (0, 0), (0, 2), (0, 0)))
    rope = _rope_tables(t, ctx_len)

    for l in range(depth):
        ctx_out = l < depth - 1
        lam_init = 0.8 - 0.6 * math.exp(-0.3 * l)
        wq = _pad_heads(mla_w_uq[l], MLA_H, MLA_NOPE + MLA_ROPE).astype(BF16)
        wkv = mla_w_ukv[l].reshape(MLA_KV_RANK, MLA_H, MLA_NOPE + MLA_V)
        wk = _pad_heads(wkv[:, :, :MLA_NOPE].reshape(MLA_KV_RANK, -1), MLA_H, MLA_NOPE).astype(BF16)
        wv = wkv[:, :, MLA_NOPE:].reshape(MLA_KV_RANK, -1).astype(BF16)
        (h, qd, kd, vd, mlqk, mlv, mlo, z, xbc, small, qm, km, vm) = _inproj(
            x_lat, x_ctx, ctx_tile0, s, mod[l], nlt, _relayout_w_in(w_in[l]), rope, mla_q_norm[l][None],
            mla_kv_norm[l][None], wq, wk, wv, tm)

        a_lat, a_ctx = _attention(qd, kd, vd, diff_lambda[l], diff_subln[l][None], diff=True,
                                  ctx_len=ctx_len, ctx_out=ctx_out, tq=tq_diff, tk=tk, lam_init=lam_init)
        m_lat, m_ctx = _attention(qm, km, vm, diff_lambda[l], diff_subln[l][None], diff=False,
                                  ctx_len=ctx_len, ctx_out=ctx_out, tq=tq_mla, tk=tk)

        ml_q, ml_k = _conv_silu(mlqk, ml_conv_w[l], ml_conv_b[l], nlt, tm,
                                (ML_H * ML_DK, ML_H * ML_DK), (1.0, ML_DK ** -0.5))
        xs, bm, cm = _conv_silu(xbc, ssd_conv_w[l], ssd_conv_b[l], nlt, tm,
                                (D_INNER, SSD_G * SSD_N, SSD_G * SSD_N), (1.0, 1.0, 1.0))
        small_t = small.T
        hf, hb, yf, yb = _scans(ml_q, ml_k.T, mlv, xs, bm.T, cm, small, small_t, ml_gate_b[l],
                                ssd_dt_bias[l], ssd_a_log[l], ctx_len, cl)

        if ctx_out:
            a_all = jnp.concatenate([a_lat, a_ctx], 0)
            m_all = jnp.concatenate([m_lat, m_ctx], 0)
            n_tiles = n
        else:
            a_all, m_all = a_lat, m_lat
            n_tiles = nlt
        x1 = _merge(x_lat, x_ctx, ctx_tile0, mod[l], nlt, n_tiles, h, a_all, hf, hb, mlo, ml_norm[l][None],
                    m_all, yf, yb, xs, z, jnp.repeat(ssd_d[l], SSD_P)[None], ssd_norm[l][None],
                    w_gate[l].astype(BF16), b_gate[l], w_branch[l].astype(BF16), w_o[l].astype(BF16),
                    ln1_g[l][None], ln1_b[l][None], tm, alpha)
        x_lat = _mlp(x1, mod[l], nlt, w_up[l].astype(BF16), b_up[l][None],
                     w_down[l].astype(BF16), b_down[l][None], ln2_g[l][None], ln2_b[l][None], tm, alpha)
        x_ctx, ctx_tile0 = x_lat, nlt
    return x_lat[None]
```

```python
import functools
import math

import jax
import jax.numpy as jnp
from jax import lax
from jax.experimental import pallas as pl
from jax.experimental.pallas import tpu as pltpu

F32 = jnp.float32
BF16 = jnp.bfloat16

GRID_W = 64
ROPE_BASE = 10000.0
CONV_K = 5
EPS = 1e-6
LOG2E = math.log2(math.e)
LANES = 128
CONV_HALO = 8
SCAN_CHUNK = 128

DIFF_H, DIFF_HD = 4, 64
ML_H, ML_DK, ML_DV = 4, 64, 128
MLA_H, MLA_Q_RANK, MLA_KV_RANK, MLA_NOPE, MLA_ROPE, MLA_V = 4, 384, 256, 64, 32, 128
SSD_H, SSD_P, SSD_G, SSD_N = 8, 64, 2, 64
D_INNER = SSD_H * SSD_P

C_QD, C_KD, C_VD, C_MLQK, C_MLV, C_MLO, C_Z, C_XBC, C_CQ, C_CKV, C_SMALL, C_END = (
    0, 512, 1024, 1536, 2048, 2560, 3072, 3584, 4352, 4736, 4992, 5120)
SM_GATE, SM_DT, SM_KR = 0, 16, 64

VMEM_LIMIT = 60 * 1024 * 1024


def _cparams(sem):
    return pltpu.CompilerParams(dimension_semantics=sem, vmem_limit_bytes=VMEM_LIMIT)


def _ln(x):
    mu = jnp.mean(x, -1, keepdims=True)
    xc = x - mu
    var = jnp.mean(xc * xc, -1, keepdims=True)
    return xc * lax.rsqrt(var + EPS)


def _sigmoid(x):
    return 0.5 * jnp.tanh(0.5 * x) + 0.5


def _softplus(x):
    return jnp.maximum(x, 0.0) + jnp.log1p(jnp.exp(-jnp.abs(x)))


def _bdot(a, b):
    return jnp.dot(a.astype(BF16), b.astype(BF16), preferred_element_type=F32)


def _hdot(a, b):
    return jnp.dot(a, b, precision=lax.Precision.HIGHEST, preferred_element_type=F32)


def _select_dot(x, sel, x_is_lhs):
    sel = sel.astype(BF16)
    acc, rest = None, x
    for _ in range(3):
        part = rest.astype(BF16)
        prod = (jnp.dot(part, sel, preferred_element_type=F32) if x_is_lhs
                else jnp.dot(sel, part, preferred_element_type=F32))
        acc = prod if acc is None else acc + prod
        rest = rest - part.astype(F32)
    return acc


def _mod_kernel(c_ref, w_ref, b_ref, o_ref):
    c = c_ref[...]
    o_ref[...] = _hdot(c * _sigmoid(c), w_ref[...]) + b_ref[...]


def _modulation(cvec, w_mod, b_mod):
    depth, d, n = w_mod.shape
    tn = 1536
    return pl.pallas_call(
        _mod_kernel,
        grid=(depth, n // tn),
        in_specs=[pl.BlockSpec((8, d), lambda l, j: (0, 0)),
                  pl.BlockSpec((None, d, tn), lambda l, j: (l, 0, j)),
                  pl.BlockSpec((None, 1, tn), lambda l, j: (l, 0, j))],
        out_specs=pl.BlockSpec((None, 8, tn), lambda l, j: (l, 0, j)),
        out_shape=jax.ShapeDtypeStruct((depth, 8, n), F32),
        compiler_params=_cparams(("arbitrary", "arbitrary")),
        name="modulation",
    )(cvec, w_mod, b_mod.reshape(depth, 1, n))


def _rope(x, cos, sin_next, sin_prev, partner):
    return (x * cos + pltpu.roll(x, LANES - partner, 1) * sin_next
            + pltpu.roll(x, partner, 1) * sin_prev)


def _rms(x, w):
    return x * lax.rsqrt(jnp.mean(x * x, -1, keepdims=True) + EPS) * w


def _inproj_kernel(xl_ref, xc_ref, mod_ref, win_ref, roper_ref, ropec_ref, qn_ref, kvn_ref, wuq_ref, wuk_ref, wuv_ref,
                   h_ref, qd_ref, kd_ref, vd_ref, mlqk_ref, mlv_ref, mlo_ref, z_ref, xbc_ref,
                   small_ref, qm_ref, km_ref, vm_ref, *, nlt):
    x = _stream_tile(xl_ref, xc_ref, nlt)
    h = _ln(x) * (1.0 + mod_ref[1:2, :]) + mod_ref[0:1, :]
    hb = h.astype(BF16)
    h_ref[...] = hb

    def proj(lo, hi):
        return jnp.dot(hb, win_ref[:, lo:hi], preferred_element_type=F32)

    col_part = jnp.where(pl.program_id(0) < nlt, ropec_ref[...], 0.0)
    rope = jnp.concatenate([roper_ref[g] + col_part for g in range(x.shape[0] // GRID_W)], 0)
    cos_d, sn_d, sp_d = rope[:, 0:128], rope[:, 128:256], rope[:, 256:384]
    cos_m, sn_m, sp_m = rope[:, 384:512], rope[:, 512:640], rope[:, 640:768]

    lane = lax.broadcasted_iota(jnp.int32, (x.shape[0], LANES), 1)
    one_col = (lane == 0).astype(BF16)

    qd = proj(C_QD, C_KD)
    kd = proj(C_KD, C_VD)
    vd = proj(C_VD, C_MLQK).astype(BF16)
    for hh in range(DIFF_H):
        sl = slice(hh * LANES, (hh + 1) * LANES)
        qd_ref[:, sl] = (_rope(qd[:, sl], cos_d, sn_d, sp_d, 16) * (DIFF_HD ** -0.5 * LOG2E)).astype(BF16)
        kd_ref[:, sl] = _rope(kd[:, sl], cos_d, sn_d, sp_d, 16).astype(BF16)
        vd_ref[:, 2 * hh * LANES:(2 * hh + 1) * LANES] = vd[:, sl]
        vd_ref[:, (2 * hh + 1) * LANES:(2 * hh + 2) * LANES] = one_col
    mlqk_ref[...] = proj(C_MLQK, C_MLV)
    mlv_ref[...] = proj(C_MLV, C_MLO).astype(BF16)
    mlo_ref[...] = proj(C_MLO, C_Z).astype(BF16)
    z_ref[...] = proj(C_Z, C_XBC).astype(BF16)
    xbc_ref[...] = proj(C_XBC, C_CQ)
    small = proj(C_SMALL, C_END)
    small_ref[...] = small

    cq = _rms(proj(C_CQ, C_CKV), qn_ref[...])
    ckv = _rms(proj(C_CKV, C_SMALL), kvn_ref[...])
    qm = _bdot(cq, wuq_ref[...])
    km = _bdot(ckv, wuk_ref[...])
    vm = _bdot(ckv, wuv_ref[...]).astype(BF16)
    kr = jnp.where((lane >= SM_KR) & (lane < SM_KR + MLA_ROPE), small, 0.0)
    kr = _rope(kr, cos_m, sn_m, sp_m, 8)
    scale = (MLA_NOPE + MLA_ROPE) ** -0.5 * LOG2E
    for hh in range(MLA_H):
        sl = slice(hh * LANES, (hh + 1) * LANES)
        qm_ref[:, sl] = (_rope(qm[:, sl], cos_m, sn_m, sp_m, 8) * scale).astype(BF16)
        km_ref[:, sl] = (km[:, sl] + kr).astype(BF16)
        vm_ref[:, 2 * hh * LANES:(2 * hh + 1) * LANES] = vm[:, sl]
        vm_ref[:, (2 * hh + 1) * LANES:(2 * hh + 2) * LANES] = one_col


def _stream_specs(tm, d, nlt, ctx_tile0):
    return [pl.BlockSpec((tm, d), lambda i: (jnp.minimum(i, nlt - 1), 0)),
            pl.BlockSpec((tm, d), lambda i: (jnp.maximum(i - nlt, 0) + ctx_tile0, 0))]


def _stream_tile(xl_ref, xc_ref, nlt):
    return jnp.where(pl.program_id(0) < nlt, xl_ref[...], xc_ref[...])


def _inproj(x_lat, x_ctx, ctx_tile0, s, mod, nlt, w_in, rope, qn, kvn, wuq, wuk, wuv, tm):
    d = x_lat.shape[1]
    n = s // tm
    rope_rows, rope_cols = rope
    row = lambda w: pl.BlockSpec((tm, w), lambda i: (i, 0))
    full = lambda a: pl.BlockSpec(a.shape, lambda i: (0,) * a.ndim)
    outs = [(d, BF16), (512, BF16), (512, BF16), (1024, BF16), (512, F32), (512, BF16), (512, BF16),
            (512, BF16), (768, F32), (128, F32), (512, BF16), (512, BF16), (1024, BF16)]
    return pl.pallas_call(
        functools.partial(_inproj_kernel, nlt=nlt),
        grid=(n,),
        in_specs=_stream_specs(tm, d, nlt, ctx_tile0) + [
            pl.BlockSpec((None, 8, d), lambda i: (jnp.where(i < nlt, 0, 1), 0, 0)),
            full(w_in), pl.BlockSpec((tm // GRID_W, 1, rope_rows.shape[2]), lambda i: (i, 0, 0)),
            full(rope_cols), full(qn), full(kvn), full(wuq), full(wuk), full(wuv)],
        out_specs=[row(w) for w, _ in outs],
        out_shape=[jax.ShapeDtypeStruct((s, w), dt) for w, dt in outs],
        compiler_params=_cparams(("arbitrary",)),
        name="inproj",
    )(x_lat, x_ctx, mod, w_in, rope_rows, rope_cols, qn, kvn, wuq, wuk, wuv)


def _conv_kernel(cur_ref, prev_ref, next_ref, w_ref, b_ref, *rest, splits, scales, nlt, n):
    out_refs, ext_ref = rest[:-1], rest[-1]
    i = pl.program_id(0)
    tm = cur_ref.shape[0]
    has_prev = jnp.logical_and(i != 0, i != nlt).astype(F32)
    has_next = jnp.logical_and(i != nlt - 1, i != n - 1).astype(F32)
    ext_ref[0:CONV_HALO, :] = prev_ref[...] * has_prev
    ext_ref[CONV_HALO:CONV_HALO + tm, :] = cur_ref[...]
    ext_ref[CONV_HALO + tm:, :] = next_ref[...] * has_next
    acc = b_ref[...] + w_ref[0:1, :] * ext_ref[pl.ds(CONV_HALO - 2, tm), :]
    for k in range(1, CONV_K):
        acc = acc + w_ref[k:k + 1, :] * ext_ref[pl.ds(CONV_HALO - 2 + k, tm), :]
    y = acc * _sigmoid(acc)
    lo = 0
    for ref, wd, sc in zip(out_refs, splits, scales):
        ref[...] = (y[:, lo:lo + wd] * sc).astype(ref.dtype)
        lo += wd


def _conv_silu(xin, w, b, nlt, tm, splits, scales):
    s, c = xin.shape
    n = s // tm
    r = tm // CONV_HALO
    wpad = jnp.zeros((8, c), F32).at[:CONV_K].set(w)
    return pl.pallas_call(
        functools.partial(_conv_kernel, splits=splits, scales=scales, nlt=nlt, n=n),
        grid=(n,),
        in_specs=[pl.BlockSpec((tm, c), lambda i: (i, 0)),
                  pl.BlockSpec((CONV_HALO, c), lambda i: (jnp.maximum(i * r - 1, 0), 0)),
                  pl.BlockSpec((CONV_HALO, c), lambda i: (jnp.minimum((i + 1) * r, n * r - 1), 0)),
                  pl.BlockSpec((8, c), lambda i: (0, 0)),
                  pl.BlockSpec((1, c), lambda i: (0, 0))],
        out_specs=[pl.BlockSpec((tm, wd), lambda i: (i, 0)) for wd in splits],
        out_shape=[jax.ShapeDtypeStruct((s, wd), BF16) for wd in splits],
        scratch_shapes=[pltpu.VMEM((tm + 2 * CONV_HALO, c), F32)],
        compiler_params=_cparams(("arbitrary",)),
        name="conv_silu",
    )(xin, xin, xin, wpad, b.reshape(1, c))


def _flash_kernel(q_ref, k_ref, v_ref, lam_ref, sub_ref, o_ref, qs_ref, s_buf, p_buf, a_buf, m_ref, acc_ref,
                  *, diff, tk, nkv, lam_init):
    tq = q_ref.shape[0]
    if diff:
        q = q_ref[...]
        lane = lax.broadcasted_iota(jnp.int32, q.shape, 1)
        zero = jnp.zeros_like(q)
        qs_ref[0:tq, :] = jnp.where(lane < DIFF_HD, q, zero)
        qs_ref[tq:, :] = jnp.where(lane >= DIFF_HD, q, zero)
    else:
        qs_ref[...] = q_ref[...]
    m_ref[...] = jnp.full(m_ref.shape, -jnp.inf, F32)
    acc_ref[...] = jnp.zeros_like(acc_ref)

    def scores(j, slot):
        off = pl.multiple_of(j * tk, tk)
        s_buf[slot] = lax.dot_general(qs_ref[...], k_ref[pl.ds(off, tk), :], (((1,), (1,)), ((), ())),
                                      preferred_element_type=F32)

    def softmax(slot):
        s = s_buf[slot]
        m_old = m_ref[...]
        m_new = jnp.maximum(m_old, jnp.max(s, -1, keepdims=True))
        p_buf[slot] = jnp.exp2(s - m_new).astype(BF16)
        a_buf[slot] = jnp.exp2(m_old - m_new)
        m_ref[...] = m_new

    def values(j, slot):
        off = pl.multiple_of(j * tk, tk)
        acc_ref[...] = a_buf[slot] * acc_ref[...] + jnp.dot(p_buf[slot], v_ref[pl.ds(off, tk), :],
                                                            preferred_element_type=F32)

    scores(0, 0)
    if nkv > 1:
        scores(1, 1)
    softmax(0)

    def step(j, slot):
        scores(j + 1, 1 - slot)
        softmax(slot)
        values(j - 1, 1 - slot)

    def body(jj, carry):
        step(2 * jj + 1, 1)
        step(2 * jj + 2, 0)
        return carry

    pairs = max(nkv - 2, 0) // 2
    if pairs:
        lax.fori_loop(0, pairs, body, 0, unroll=True)
    if max(nkv - 2, 0) % 2:
        step(nkv - 2, (nkv - 2) % 2)
    if nkv > 1:
        softmax((nkv - 1) % 2)
        values(nkv - 2, nkv % 2)
    values(nkv - 1, (nkv - 1) % 2)

    acc = acc_ref[...]
    o = acc[:, :LANES] / acc[:, LANES:LANES + 1]
    if diff:
        lam = lam_ref[...]
        lam_full = (jnp.exp(jnp.sum(lam[0:1, :] * lam[1:2, :], -1, keepdims=True))
                    - jnp.exp(jnp.sum(lam[2:3, :] * lam[3:4, :], -1, keepdims=True)) + lam_init)
        o = o[:tq] - lam_full * o[tq:]
        o = _rms(o, sub_ref[...]) * (1.0 - lam_init)
    o_ref[...] = o.astype(o_ref.dtype)


def _flash(q, k, v, lam, subln, *, diff, q_tile0, nq, kv_block, kv_len, tq, tk, lam_init=0.0):
    heads = q.shape[1] // LANES
    rows = 2 * tq if diff else tq
    return pl.pallas_call(
        functools.partial(_flash_kernel, diff=diff, tk=tk, nkv=kv_len // tk, lam_init=lam_init),
        grid=(heads, nq),
        in_specs=[pl.BlockSpec((tq, LANES), lambda h, i: (i + q_tile0, h)),
                  pl.BlockSpec((kv_len, LANES), lambda h, i: (kv_block, h)),
                  pl.BlockSpec((kv_len, 2 * LANES), lambda h, i: (kv_block, h)),
                  pl.BlockSpec(lam.shape, lambda h, i: (0, 0)),
                  pl.BlockSpec(subln.shape, lambda h, i: (0, 0))],
        out_specs=pl.BlockSpec((tq, LANES), lambda h, i: (i, h)),
        out_shape=jax.ShapeDtypeStruct((nq * tq, heads * LANES), BF16),
        scratch_shapes=[pltpu.VMEM((rows, LANES), BF16),
                        pltpu.VMEM((2, rows, tk), F32),
                        pltpu.VMEM((2, rows, tk), BF16),
                        pltpu.VMEM((2, rows, 1), F32),
                        pltpu.VMEM((rows, 1), F32),
                        pltpu.VMEM((rows, 2 * LANES), F32)],
        compiler_params=_cparams(("arbitrary", "arbitrary")),
        name="flash_diff" if diff else "flash_mla",
    )(q, k, v, lam, subln)


def _attention(q, k, v, lam, subln, *, diff, ctx_len, ctx_out, tq, tk, lam_init=0.0):
    s = q.shape[0]
    t = s - ctx_len
    lat = _flash(q, k, v, lam, subln, diff=diff, q_tile0=0, nq=t // tq, kv_block=0, kv_len=s,
                 tq=tq, tk=tk, lam_init=lam_init)
    if not ctx_out:
        return lat, None
    tqc = math.gcd(tq, ctx_len)
    ctx = _flash(q, k, v, lam, subln, diff=diff, q_tile0=t // tqc, nq=ctx_len // tqc, kv_block=t // ctx_len,
                 kv_len=ctx_len, tq=tqc, tk=math.gcd(tk, ctx_len), lam_init=lam_init)
    return lat, ctx


def _tri(n, upper):
    r = lax.broadcasted_iota(jnp.int32, (n, n), 0)
    c = lax.broadcasted_iota(jnp.int32, (n, n), 1)
    return (c >= r) if upper else (c <= r)


def _mlstm_dir(bwd, q_ref, kt_ref, v_ref, sm_ref, smt_ref, gb_ref, gbt_ref, o_ref, c_ref, m_ref):
    cl = q_ref.shape[0]
    assert cl == LANES == ML_DV
    allowed = _tri(cl, bwd)
    pre = sm_ref[...] + gb_ref[...]
    pre_t = smt_ref[...] + gbt_ref[...]
    lf = -_softplus(-pre)
    lf_t = -_softplus(-pre_t)
    g_col = _select_dot(lf, allowed, False)
    g_row = _select_dot(lf_t, _tri(cl, not bwd), True)
    q = q_ref[...]
    kt = kt_ref[...]
    lane_q = lax.broadcasted_iota(jnp.int32, q.shape, 1)
    ones = jnp.ones((cl, LANES), BF16)
    end = 0 if bwd else cl - 1
    twice = lambda a: jnp.concatenate([a, a], 1)
    for hh in range(ML_H):
        ci, cf = (2 * bwd) * ML_H + hh, (2 * bwd + 1) * ML_H + hh
        g_c = jnp.broadcast_to(g_col[:, cf:cf + 1], (cl, LANES))
        i_c = jnp.broadcast_to(pre[:, ci:ci + 1], (cl, LANES))
        g_r, i_r = g_row[cf:cf + 1, :], pre_t[ci:ci + 1, :]
        m_prev = m_ref[hh:hh + 1, :]
        qh = jnp.where((lane_q >= hh * ML_DK) & (lane_q < (hh + 1) * ML_DK), q, jnp.zeros_like(q))
        vh = jnp.concatenate([v_ref[:, hh * ML_DV:(hh + 1) * ML_DV], ones], 1)
        logw = jnp.where(allowed, g_c - g_r + i_r, -jnp.inf)
        log_inter = g_c + m_prev
        m_loc = jnp.broadcast_to(jnp.max(logw, -1, keepdims=True), (cl, LANES))
        m_j = jnp.maximum(log_inter, m_loc)
        w = jnp.exp(logw - m_j) * jnp.dot(qh, kt, preferred_element_type=F32)
        a_inter = jnp.exp(log_inter - m_j)
        cstate = c_ref[...]
        nd = _bdot(w, vh) + twice(a_inter) * _bdot(qh, cstate)
        den = jnp.broadcast_to(nd[:, ML_DV:ML_DV + 1], (cl, LANES))
        o_ref[:, hh * ML_DV:(hh + 1) * ML_DV] = nd[:, :ML_DV] / jnp.maximum(jnp.abs(den), jnp.exp(-m_j))
        g_end = g_c[end:end + 1, :]
        log_s = g_end - g_c + i_c
        m_new = jnp.maximum(g_end + m_prev, jnp.max(log_s, 0, keepdims=True))
        ws = jnp.exp(log_s - m_new)
        decay = jnp.exp(g_end + m_prev - m_new)
        rows = slice(hh * ML_DK, (hh + 1) * ML_DK)
        upd = jnp.dot(kt[rows, :], (twice(ws) * vh.astype(F32)).astype(BF16), preferred_element_type=F32)
        c_ref[rows, :] = twice(decay) * cstate[rows, :] + upd
        m_ref[hh:hh + 1, :] = m_new


def _ssd_dir(bwd, xs_ref, bt_ref, cm_ref, sm_ref, smt_ref, db_ref, dbt_ref, al_ref, alt_ref, ex_ref,
             o_ref, st_ref):
    cl = xs_ref.shape[0]
    allowed = _tri(cl, bwd)
    dt = _softplus(sm_ref[...] + db_ref[...])
    dt_t = _softplus(smt_ref[...] + dbt_ref[...])
    a = dt * (-jnp.exp(al_ref[...]))
    a_t = dt_t * (-jnp.exp(alt_ref[...]))
    s_col = _select_dot(a, allowed, False)
    s_row = _select_dot(a_t, _tri(cl, not bwd), True)
    end = 0 if bwd else cl - 1
    s_end = s_col[end:end + 1, :]
    stack = jnp.concatenate([jnp.exp(s_col), dt, dt * jnp.exp(s_end - s_col), jnp.exp(jnp.broadcast_to(s_end, (8, LANES)))], 0)
    wide = _select_dot(stack, ex_ref[bwd], True)
    es_w, dt_w, wt_w, dec_w = wide[:cl], wide[cl:2 * cl], wide[2 * cl:3 * cl], wide[3 * cl:3 * cl + 1]
    xs = xs_ref[...].astype(F32)
    xdt = (xs * dt_w).astype(BF16)
    cm = cm_ref[...]
    bt = bt_ref[...]
    lane_c = lax.broadcasted_iota(jnp.int32, cm.shape, 1)
    lane_x = lax.broadcasted_iota(jnp.int32, (cl, LANES), 1)
    state = st_ref[...]
    y = es_w * _bdot(cm, state)
    per_pair = SSD_H // (D_INNER // LANES)
    cbs = []
    for grp in range(SSD_G):
        cg = jnp.where((lane_c >= grp * SSD_N) & (lane_c < (grp + 1) * SSD_N), cm, jnp.zeros_like(cm))
        cbs.append(jnp.dot(cg, bt, preferred_element_type=F32))
    for blk in range(D_INNER // LANES):
        cb = cbs[(blk * per_pair) // (SSD_H // SSD_G)]
        xblk = xdt[:, blk * LANES:(blk + 1) * LANES]
        res = []
        for sub in range(per_pair):
            hh = blk * per_pair + sub
            col = SM_DT + bwd * SSD_H + hh
            decay = jnp.exp(jnp.where(allowed, s_col[:, col:col + 1] - s_row[col:col + 1, :], -jnp.inf))
            res.append(_bdot(decay * cb, xblk))
        intra = jnp.where(lane_x < SSD_P, res[0], res[1])
        o_ref[:, blk * LANES:(blk + 1) * LANES] = y[:, blk * LANES:(blk + 1) * LANES] + intra
    upd = jnp.dot(bt, (xs * wt_w).astype(BF16), preferred_element_type=F32)
    r = lax.broadcasted_iota(jnp.int32, upd.shape, 0) // SSD_N
    c = lax.broadcasted_iota(jnp.int32, upd.shape, 1) // (SSD_P * (SSD_H // SSD_G))
    st_ref[...] = jnp.where(r == c, dec_w * state + upd, 0.0)


def _scan_kernel(qf, ktf, vf, xf, btf, cmf, smf, smtf, qb, ktb, vb, xb, btb, cmb, smb, smtb,
                 gb_ref, gbt_ref, db_ref, dbt_ref, al_ref, alt_ref, ex_ref,
                 hf_ref, hb_ref, yf_ref, yb_ref, cf_ref, mf_ref, cb_ref, mb_ref, sf_ref, sb_ref):
    @pl.when(pl.program_id(0) == 0)
    def _():
        for ref in (cf_ref, mf_ref, cb_ref, mb_ref, sf_ref, sb_ref):
            ref[...] = jnp.zeros_like(ref)

    cl = SCAN_CHUNK
    per_step = qf.shape[0] // cl
    rows = lambda ref, k: ref.at[pl.ds(k * cl, cl), :]
    cols = lambda ref, k: ref.at[:, pl.ds(k * cl, cl)]
    for sub in range(per_step):
        f, b = sub, per_step - 1 - sub
        _mlstm_dir(0, rows(qf, f), cols(ktf, f), rows(vf, f), rows(smf, f), cols(smtf, f), gb_ref, gbt_ref,
                   rows(hf_ref, f), cf_ref, mf_ref)
        _ssd_dir(0, rows(xf, f), cols(btf, f), rows(cmf, f), rows(smf, f), cols(smtf, f), db_ref, dbt_ref,
                 al_ref, alt_ref, ex_ref, rows(yf_ref, f), sf_ref)
        _mlstm_dir(1, rows(qb, b), cols(ktb, b), rows(vb, b), rows(smb, b), cols(smtb, b), gb_ref, gbt_ref,
                   rows(hb_ref, b), cb_ref, mb_ref)
        _ssd_dir(1, rows(xb, b), cols(btb, b), rows(cmb, b), rows(smb, b), cols(smtb, b), db_ref, dbt_ref,
                 al_ref, alt_ref, ex_ref, rows(yb_ref, b), sb_ref)


def _scan_orders(nc, ncc):
    nlc = nc - ncc
    fwd = lambda c: jnp.where(c < ncc, nlc + c, c - ncc)
    bwd = lambda c: nc - 1 - c
    return fwd, bwd


def _scans(q, kt, v, xs, bt, cm, small, small_t, gate_b, dt_bias, a_log, ctx_len, cl):
    s = xs.shape[0]
    nc, ncc = s // cl, ctx_len // cl
    gb = jnp.zeros((1, LANES), F32).at[0, SM_GATE:SM_GATE + 4 * ML_H].set(gate_b.reshape(-1))
    db = jnp.zeros((1, LANES), F32).at[0, SM_DT:SM_DT + 2 * SSD_H].set(dt_bias.reshape(-1))
    al = jnp.zeros((1, LANES), F32).at[0, SM_DT:SM_DT + 2 * SSD_H].set(a_log.reshape(-1))
    src = jnp.arange(LANES)[None, :, None]
    dst_head = (jnp.arange(D_INNER) // SSD_P)[None, None, :]
    dirs = jnp.arange(2)[:, None, None]
    expand = (src == SM_DT + dirs * SSD_H + dst_head).astype(BF16)

    def specs(order):
        rows = lambda w: pl.BlockSpec((cl, w), lambda c: (order(c), 0))
        cols = lambda h: pl.BlockSpec((h, cl), lambda c: (0, order(c)))
        return [rows(ML_H * ML_DK), cols(ML_H * ML_DK), rows(ML_H * ML_DV),
                rows(D_INNER), cols(SSD_G * SSD_N), rows(SSD_G * SSD_N), rows(LANES), cols(LANES)]

    fwd, bwd = _scan_orders(nc, ncc)
    const = lambda a: pl.BlockSpec(a.shape, lambda c: (0,) * a.ndim)
    gbt, dbt, alt = gb.reshape(LANES, 1), db.reshape(LANES, 1), al.reshape(LANES, 1)
    consts = [gb, gbt, db, dbt, al, alt, expand]
    ml_state = [pltpu.VMEM((ML_H * ML_DK, 2 * ML_DV), F32), pltpu.VMEM((8, LANES), F32)]
    ssd_state = pltpu.VMEM((SSD_G * SSD_N, D_INNER), F32)
    out = lambda order, w: pl.BlockSpec((cl, w), lambda c: (order(c), 0))
    operands = (q, kt, v, xs, bt, cm, small, small_t)
    return pl.pallas_call(
        _scan_kernel,
        grid=(nc,),
        in_specs=specs(fwd) + specs(bwd) + [const(a) for a in consts],
        out_specs=[out(fwd, ML_H * ML_DV), out(bwd, ML_H * ML_DV), out(fwd, D_INNER), out(bwd, D_INNER)],
        out_shape=[jax.ShapeDtypeStruct((s, ML_H * ML_DV), F32)] * 2
        + [jax.ShapeDtypeStruct((s, D_INNER), F32)] * 2,
        scratch_shapes=ml_state + ml_state + [ssd_state, ssd_state],
        compiler_params=_cparams(("arbitrary",)),
        name="scans",
    )(*operands, *operands, *consts)


def _merge_kernel(xl_ref, xc_ref, al_ref, ac_ref, ml_ref, mc_ref, mod_ref, h_ref, hf_ref, hb_ref, mlo_ref,
                  mln_ref, yf_ref, yb_ref, xs_ref, z_ref, dsk_ref, sdn_ref, wg_ref, bg_ref, wb_ref, wo_ref,
                  g_ref, b_ref, o_ref, *, alpha, nlt):
    hb16 = h_ref[...]
    hm = hf_ref[...] + hb_ref[...]
    og = _sigmoid(mlo_ref[...].astype(F32))
    b_parts = []
    for hh in range(ML_H):
        sl = slice(hh * ML_DV, (hh + 1) * ML_DV)
        b_parts.append(_rms(hm[:, sl], mln_ref[:, sl]) * og[:, sl])
    b_br = jnp.concatenate(b_parts, 1)
    z = z_ref[...].astype(F32)
    ys = (yf_ref[...] + yb_ref[...] + dsk_ref[...] * xs_ref[...].astype(F32)) * (z * _sigmoid(z))
    gw = D_INNER // SSD_G
    s_br = jnp.concatenate([_rms(ys[:, g * gw:(g + 1) * gw], sdn_ref[:, g * gw:(g + 1) * gw])
                            for g in range(SSD_G)], 1)
    branches = (_stream_tile(al_ref, ac_ref, nlt), b_br, _stream_tile(ml_ref, mc_ref, nlt), s_br)
    y = None
    for k, br in enumerate(branches):
        gate = _sigmoid(jnp.dot(hb16, wg_ref[k], preferred_element_type=F32) + bg_ref[k:k + 1, :])
        term = gate * _bdot(br, wb_ref[k])
        y = term if y is None else y + term
    y = _bdot(y, wo_ref[...])
    x1 = alpha * _stream_tile(xl_ref, xc_ref, nlt) + mod_ref[2:3, :] * y
    o_ref[...] = _ln(x1) * g_ref[...] + b_ref[...]


def _merge(x_lat, x_ctx, ctx_tile0, a_lat, a_ctx, m_lat, m_ctx, mod, nlt, n_tiles, h, hf, hb, mlo, mln, yf, yb,
           xs, z, dsk, sdn, wg, bg, wb, wo, g, b, tm, alpha):
    d = x_lat.shape[1]
    row = lambda w: pl.BlockSpec((tm, w), lambda i: (i, 0))
    full = lambda arr: pl.BlockSpec(arr.shape, lambda i: (0,) * arr.ndim)
    pair = _stream_specs(tm, a_lat.shape[1], nlt, 0)
    return pl.pallas_call(
        functools.partial(_merge_kernel, alpha=alpha, nlt=nlt),
        grid=(n_tiles,),
        in_specs=_stream_specs(tm, d, nlt, ctx_tile0) + pair + pair + [
            pl.BlockSpec((None, 8, d), lambda i: (jnp.where(i < nlt, 0, 1), 0, 0)),
            row(d), row(512), row(512), row(512), full(mln), row(512), row(512),
            row(512), row(512), full(dsk), full(sdn), full(wg), full(bg), full(wb), full(wo),
            full(g), full(b)],
        out_specs=row(d),
        out_shape=jax.ShapeDtypeStruct((n_tiles * tm, d), F32),
        compiler_params=_cparams(("arbitrary",)),
        name="merge_ln1",
    )(x_lat, x_ctx, a_lat, a_ctx, m_lat, m_ctx, mod, h, hf, hb, mlo, mln, yf, yb, xs, z, dsk, sdn, wg, bg, wb,
      wo, g, b)


def _mlp_kernel(x_ref, mod_ref, wu_ref, bu_ref, wd_ref, bd_ref, g_ref, b_ref, o_ref, *, alpha):
    x = x_ref[...]
    hm = _ln(x) * (1.0 + mod_ref[4:5, :]) + mod_ref[3:4, :]
    u = jnp.maximum(_bdot(hm, wu_ref[...]) + bu_ref[...], 0.0)
    f = _bdot(u * u, wd_ref[...]) + bd_ref[...]
    x2 = alpha * x + mod_ref[5:6, :] * f
    o_ref[...] = _ln(x2) * g_ref[...] + b_ref[...]


def _mlp(x1, mod, nlt, wu, bu, wd, bd, g, b, tm, alpha):
    s, d = x1.shape
    n_tiles = s // tm
    full = lambda arr: pl.BlockSpec(arr.shape, lambda i: (0,) * arr.ndim)
    return pl.pallas_call(
        functools.partial(_mlp_kernel, alpha=alpha),
        grid=(n_tiles,),
        in_specs=[pl.BlockSpec((tm, d), lambda i: (i, 0)),
                  pl.BlockSpec((None, 8, d), lambda i: (jnp.where(i < nlt, 0, 1), 0, 0)),
                  full(wu), full(bu), full(wd), full(bd), full(g), full(b)],
        out_specs=pl.BlockSpec((tm, d), lambda i: (i, 0)),
        out_shape=jax.ShapeDtypeStruct((n_tiles * tm, d), F32),
        compiler_params=_cparams(("arbitrary",)),
        name="mlp_ln2",
    )(x1, mod, wu, bu, wd, bd, g, b)


def _axis_tables(pos, dsub):
    half = dsub // 2
    inv = ROPE_BASE ** (-jnp.arange(half, dtype=F32) * 2.0 / dsub)
    ang = pos[:, None] * inv[None, :]
    c, s = jnp.cos(ang), jnp.sin(ang)
    z = jnp.zeros_like(s)
    return jnp.concatenate([c, c], -1), jnp.concatenate([-s, z], -1), jnp.concatenate([z, s], -1)


def _rope_tables(t, ctx_len):
    n_rows = t // GRID_W
    pad = LANES - MLA_NOPE - MLA_ROPE

    def parts(pos, is_row):
        n = pos.shape[0]
        zeros = lambda w: jnp.zeros((n, w), F32)
        fill = lambda w, cos: jnp.ones((n, w), F32) if (cos and is_row) else zeros(w)
        out = []
        for k, a in enumerate(_axis_tables(pos, DIFF_HD // 2)):
            one_map = jnp.concatenate([a, zeros(DIFF_HD // 2)] if is_row else [zeros(DIFF_HD // 2), a], -1)
            out.append(jnp.tile(one_map, (1, 2)))
        for k, a in enumerate(_axis_tables(pos, MLA_ROPE // 2)):
            rot = jnp.concatenate([a, zeros(MLA_ROPE // 2)] if is_row else [zeros(MLA_ROPE // 2), a], -1)
            out.append(jnp.concatenate([fill(MLA_NOPE, k == 0), rot, fill(pad, k == 0)], -1))
        return jnp.concatenate(out, -1)

    rows = parts(jnp.arange(n_rows, dtype=F32), True)
    cols = parts(jnp.arange(GRID_W, dtype=F32), False)
    n_ctx = ctx_len // GRID_W
    ident = jnp.concatenate([jnp.ones((n_ctx, LANES), F32), jnp.zeros((n_ctx, 2 * LANES), F32)] * 2, -1)
    return jnp.concatenate([rows, ident], 0)[:, None, :], cols


def _relayout_w_in(w):
    d = w.shape[0]
    o = [0, 512, 1024, 1536, 2048, 2560, 3072, 3088, 3472, 3728, 3760, 4272, 5040, 5056]
    seg = lambda i: w[:, o[i]:o[i + 1]]
    qd, kd, vd, mlqk, mlv, mlo, mlg, cq, ckv, kr, z, xbc, dt = [seg(i) for i in range(13)]
    zc = lambda n: jnp.zeros((d, n), w.dtype)
    small = jnp.concatenate([mlg, dt, zc(SM_KR - 32), kr, zc(LANES - SM_KR - MLA_ROPE)], 1)
    return jnp.concatenate([qd, kd, vd, mlqk, mlv, mlo, z, xbc, cq, ckv, small], 1).astype(BF16)


def _pad_heads(w, heads, width):
    r = w.shape[0]
    w = w.reshape(r, heads, width)
    return jnp.pad(w, ((0, 0), (0, 0), (0, LANES - width))).reshape(r, heads * LANES)


def _pick_tile(n, pref):
    for t in pref:
        if n % t == 0:
            return t
    raise ValueError(f"no tile for {n}")


def kernel(x, c, ctx, c_ctx, w_mod, b_mod, w_in, diff_lambda, diff_subln, ml_conv_w, ml_conv_b, ml_gate_b, ml_norm, mla_q_norm, mla_kv_norm, mla_w_uq, mla_w_ukv, ssd_conv_w, ssd_conv_b, ssd_dt_bias, ssd_a_log, ssd_d, ssd_norm, w_gate, b_gate, w_branch, w_o, ln1_g, ln1_b, w_up, b_up, w_down, b_down, ln2_g, ln2_b):
    assert x.shape[0] == 1 and ctx.shape[0] == 1
    depth = w_in.shape[0]
    t, d = x.shape[1], x.shape[2]
    ctx_len = ctx.shape[1]
    s = ctx_len + t
    tm = _pick_tile(math.gcd(ctx_len, t), (256, 128))
    cl = _pick_tile(math.gcd(ctx_len, t), (2 * SCAN_CHUNK, SCAN_CHUNK))
    tq_diff = _pick_tile(t, (256, 128))
    tq_mla = _pick_tile(t, (512, 256, 128))
    tk = _pick_tile(s, (3328, 1280, 640, 256, 128))
    assert ctx_len % cl == 0 and t % cl == 0 and t % GRID_W == 0 and t % ctx_len == 0
    nlt, n = t // tm, s // tm
    alpha = (2 * depth) ** 0.25

    x_lat, x_ctx, ctx_tile0 = x[0], ctx[0], 0
    cvec = jnp.zeros((8, d), F32).at[0].set(c[0]).at[1].set(c_ctx)
    mod = _modulation(cvec, w_mod, b_mod)
    mod = jnp.pad(mod[:, :2].reshape(depth, 2, 6, d), ((0, 0), (0, 0), (0, 2), (0, 0)))
    rope = _rope_tables(t, ctx_len)

    for l in range(depth):
        ctx_out = l < depth - 1
        lam_init = 0.8 - 0.6 * math.exp(-0.3 * l)
        wq = _pad_heads(mla_w_uq[l], MLA_H, MLA_NOPE + MLA_ROPE).astype(BF16)
        wkv = mla_w_ukv[l].reshape(MLA_KV_RANK, MLA_H, MLA_NOPE + MLA_V)
        wk = _pad_heads(wkv[:, :, :MLA_NOPE].reshape(MLA_KV_RANK, -1), MLA_H, MLA_NOPE).astype(BF16)
        wv = wkv[:, :, MLA_NOPE:].reshape(MLA_KV_RANK, -1).astype(BF16)
        (h, qd, kd, vd, mlqk, mlv, mlo, z, xbc, small, qm, km, vm) = _inproj(
            x_lat, x_ctx, ctx_tile0, s, mod[l], nlt, _relayout_w_in(w_in[l]), rope, mla_q_norm[l][None],
            mla_kv_norm[l][None], wq, wk, wv, tm)

        a_lat, a_ctx = _attention(qd, kd, vd, diff_lambda[l], diff_subln[l][None], diff=True,
                                  ctx_len=ctx_len, ctx_out=ctx_out, tq=tq_diff, tk=tk, lam_init=lam_init)
        m_lat, m_ctx = _attention(qm, km, vm, diff_lambda[l], diff_subln[l][None], diff=False,
                                  ctx_len=ctx_len, ctx_out=ctx_out, tq=tq_mla, tk=tk)

        ml_q, ml_k = _conv_silu(mlqk, ml_conv_w[l], ml_conv_b[l], nlt, tm,
                                (ML_H * ML_DK, ML_H * ML_DK), (1.0, ML_DK ** -0.5))
        xs, bm, cm = _conv_silu(xbc, ssd_conv_w[l], ssd_conv_b[l], nlt, tm,
                                (D_INNER, SSD_G * SSD_N, SSD_G * SSD_N), (1.0, 1.0, 1.0))
        small_t = small.T
        hf, hb, yf, yb = _scans(ml_q, ml_k.T, mlv, xs, bm.T, cm, small, small_t, ml_gate_b[l],
                                ssd_dt_bias[l], ssd_a_log[l], ctx_len, cl)

        if ctx_out:
            n_tiles = n
        else:
            a_ctx, m_ctx, n_tiles = a_lat, m_lat, nlt
        x1 = _merge(x_lat, x_ctx, ctx_tile0, a_lat, a_ctx, m_lat, m_ctx, mod[l], nlt, n_tiles, h, hf, hb, mlo,
                    ml_norm[l][None], yf, yb, xs, z, jnp.repeat(ssd_d[l], SSD_P)[None], ssd_norm[l][None],
                    w_gate[l].astype(BF16), b_gate[l], w_branch[l].astype(BF16), w_o[l].astype(BF16),
                    ln1_g[l][None], ln1_b[l][None], tm, alpha)
        x_lat = _mlp(x1, mod[l], nlt, w_up[l].astype(BF16), b_up[l][None],
                     w_down[l].astype(BF16), b_down[l][None], ln2_g[l][None], ln2_b[l][None], tm, alpha)
        x_ctx, ctx_tile0 = x_lat, nlt
    return x_lat[None]
```
